```python
import jax, jax.numpy as jnp
from jax import lax
import numpy as np

D_MODEL = 4096
BATCH = 4
SEQ = 2048
DEPTH = 2
DEC_BATCH = 8
DEC_SEQ = 4
PAST_LEN = 16384
PAGE_SIZE = 128

N_A = DEPTH // 2
N_B = DEPTH - N_A
MIX_WIDTH = D_MODEL
MEM_LEN = 256
MEM_HEADS = 4
MEM_WIDTH = MIX_WIDTH // 4
MEM_HEAD_DIM = MEM_WIDTH // MEM_HEADS
TOK_WIDTH = MIX_WIDTH - MEM_WIDTH
HG_HEAD_DIM = 128
HG_HEADS = TOK_WIDTH // HG_HEAD_DIM
HG_CHUNK = 16
FOX_HEAD_DIM = 128
FOX_HEADS = TOK_WIDTH // FOX_HEAD_DIM
Q_BLOCK = 128
D_FF = 4 * D_MODEL
FGATE_BIAS_INIT = 2.0
NORM_EPS = 1e-6

kernel_name = 'yoco_hgrn2_fox_memory_decoder_step'

F32 = jnp.float32


def rms_norm(x, gain):
    xf = x.astype(F32)
    y = xf * lax.rsqrt(jnp.mean(xf * xf, axis=-1, keepdims=True) + NORM_EPS)
    return (y * gain.astype(F32)).astype(x.dtype)


def gla_chunked(q, k, v, g, s0):
    B, T, H, K = q.shape
    V = v.shape[-1]
    C = HG_CHUNK if T % HG_CHUNK == 0 else T
    n = T // C

    def split(a):
        return a.astype(F32).reshape(B, n, C, H, a.shape[-1]).transpose(1, 0, 3, 2, 4)

    tri = jnp.tril(jnp.ones((C, C), dtype=bool))[:, :, None]

    def step(S, inp):
        qc, kc, vc, gc = inp
        b = jnp.cumsum(gc, axis=2)
        o_inter = jnp.einsum('bhck,bhkv->bhcv', qc * jnp.exp(b), S)
        decay = jnp.exp(jnp.where(tri, b[:, :, :, None, :] - b[:, :, None, :, :], -jnp.inf))
        att = jnp.einsum('bhtk,bhsk,bhtsk->bhts', qc, kc, decay)
        o_intra = jnp.einsum('bhts,bhsv->bhtv', att, vc)
        b_last = b[:, :, -1, :]
        S_new = jnp.exp(b_last)[..., None] * S + jnp.einsum(
            'bhck,bhcv->bhkv', kc * jnp.exp(b_last[:, :, None, :] - b), vc)
        return S_new, o_inter + o_intra

    S, o = lax.scan(step, s0.astype(F32), (split(q), split(k), split(v), split(g)))
    return o.transpose(1, 0, 3, 2, 4).reshape(B, T, H, V), S


def hgrn2_mixer(zq, zf, zi, zg, lb, out_norm, s0):
    B, T, _ = zq.shape
    shp = (B, T, HG_HEADS, HG_HEAD_DIM)
    q = jax.nn.silu(zq.astype(F32)).reshape(shp)
    f = lb + (1.0 - lb) * jax.nn.sigmoid(zf.astype(F32))
    logf = jnp.log(f).reshape(shp)
    k = (1.0 - f).reshape(shp)
    i = zi.astype(F32).reshape(shp)
    o, S = gla_chunked(q, k, i, logf, s0)
    o = rms_norm(o, out_norm.reshape(HG_HEADS, HG_HEAD_DIM)).reshape(B, T, TOK_WIDTH)
    return (o * jax.nn.silu(zg.astype(F32))).astype(zq.dtype), S


def fox_attention(q, c_q, q_offset, segments):
    B, Tq, H, D = q.shape
    qb = Q_BLOCK if Tq % Q_BLOCK == 0 else Tq
    nb = Tq // qb
    qs = q.reshape(B, nb, qb, H, D).swapaxes(0, 1)
    cqs = c_q.reshape(B, nb, qb, H).swapaxes(0, 1)
    scale = FOX_HEAD_DIM ** -0.5
    segs = [(k, v, ck.transpose(0, 2, 1), start + jnp.arange(k.shape[1])) for (k, v, ck, start) in segments]
    lens = [k.shape[1] for (k, _, _, _) in segments]

    def block(args):
        qi, cqi, bi = args
        q_pos = q_offset + bi * qb + jnp.arange(qb)
        cq_t = cqi.transpose(0, 2, 1)[..., None]
        scores = []
        for k, v, ckT, k_pos in segs:
            s = jnp.einsum('bqhd,bkhd->bhqk', qi, k).astype(F32) * scale + cq_t - ckT[:, :, None, :]
            scores.append(jnp.where(k_pos[None, :] <= q_pos[:, None], s, -jnp.inf))
        p = jax.nn.softmax(jnp.concatenate(scores, axis=-1), axis=-1)
        outs = []
        off = 0
        for (k, v, ckT, k_pos), L in zip(segs, lens):
            outs.append(jnp.einsum('bhqk,bkhd->bqhd', p[..., off:off + L].astype(v.dtype), v))
            off += L
        out = outs[0]
        for o in outs[1:]:
            out = out + o
        return out

    o = lax.map(block, (qs, cqs, jnp.arange(nb)))
    return o.swapaxes(0, 1).reshape(B, Tq, H, D)


def memory_attention(q, mk, mv):
    s = jnp.einsum('bthd,bmhd->bhtm', q, mk).astype(F32) * (MEM_HEAD_DIM ** -0.5)
    p = jax.nn.softmax(s, axis=-1).astype(mv.dtype)
    return jnp.einsum('bhtm,bmhd->bthd', p, mv)


def memory_kv(mem, mem_norm, w_mem_kv):
    kv = jnp.einsum('bmd,lde->lbme', rms_norm(mem, mem_norm), w_mem_kv)
    L, B, M, _ = kv.shape
    k = kv[..., :MEM_WIDTH].reshape(L, B, M, MEM_HEADS, MEM_HEAD_DIM)
    v = kv[..., MEM_WIDTH:].reshape(L, B, M, MEM_HEADS, MEM_HEAD_DIM)
    return k, v


def trunk(x, q_offset, mem_k, mem_v, hg_s0, past_k, past_v, past_logf,
          norm_pre_mix, norm_post_mix, norm_pre_mlp, norm_post_mlp,
          w_in_a, hg_lb_logits, hg_out_norm, w_in_b, kv_norm, w_kv_shared, b_fgate,
          w_out, w_mlp_up, w_mlp_down):
    B, T, _ = x.shape
    lb_all = jnp.cumsum(jax.nn.softmax(hg_lb_logits.astype(F32), axis=0), axis=0)
    h = x
    hg_states = []
    segments = None
    c_new = None
    k_new = v_new = logf_new = None
    for layer in range(DEPTH):
        if layer == N_A:
            kvf = rms_norm(h, kv_norm) @ w_kv_shared
            k_new = kvf[..., :TOK_WIDTH].reshape(B, T, FOX_HEADS, FOX_HEAD_DIM)
            v_new = kvf[..., TOK_WIDTH:2 * TOK_WIDTH].reshape(B, T, FOX_HEADS, FOX_HEAD_DIM)
            logf_new = jax.nn.log_sigmoid(kvf[..., 2 * TOK_WIDTH:].astype(F32) + b_fgate.astype(F32))
            if past_k is None:
                c_new = jnp.cumsum(logf_new, axis=1)
                segments = [(k_new, v_new, c_new, 0)]
            else:
                c_past = jnp.cumsum(past_logf.astype(F32), axis=1)
                c_new = c_past[:, -1:, :] + jnp.cumsum(logf_new, axis=1)
                segments = [(past_k, past_v, c_past, 0), (k_new, v_new.astype(past_v.dtype), c_new, past_k.shape[1])]
        hn = rms_norm(h, norm_pre_mix[layer])
        if layer < N_A:
            z = hn @ w_in_a[layer]
            o_tok, s_fin = hgrn2_mixer(z[..., :TOK_WIDTH], z[..., TOK_WIDTH:2 * TOK_WIDTH],
                                       z[..., 2 * TOK_WIDTH:3 * TOK_WIDTH], z[..., 3 * TOK_WIDTH:4 * TOK_WIDTH],
                                       lb_all[layer], hg_out_norm[layer], hg_s0[layer])
            hg_states.append(s_fin)
            zm = z[..., 4 * TOK_WIDTH:]
        else:
            z = hn @ w_in_b[layer - N_A]
            q = z[..., :TOK_WIDTH].reshape(B, T, FOX_HEADS, FOX_HEAD_DIM)
            o_tok = fox_attention(q, c_new, q_offset, segments).reshape(B, T, TOK_WIDTH).astype(x.dtype)
            zm = z[..., TOK_WIDTH:]
        o_mem = memory_attention(zm.reshape(B, T, MEM_HEADS, MEM_HEAD_DIM), mem_k[layer], mem_v[layer])
        mixed = jnp.concatenate([o_tok, o_mem.reshape(B, T, MEM_WIDTH).astype(o_tok.dtype)], axis=-1) @ w_out[layer]
        h = h + rms_norm(mixed, norm_post_mix[layer])
        u = jnp.square(jax.nn.relu(rms_norm(h, norm_pre_mlp[layer]) @ w_mlp_up[layer]))
        h = h + rms_norm(u @ w_mlp_down[layer], norm_post_mlp[layer])
    return h, jnp.stack(hg_states), k_new, v_new, logf_new


def setup_inputs(seed: int = 0) -> dict:
    key = jax.random.key(seed)
    ks = jax.random.split(key, 32)
    n_pages = PAST_LEN // PAGE_SIZE
    n_used = DEC_BATCH * n_pages
    n_pool = n_used + n_used // 4

    def nrm(k, shape, scale=1.0):
        return jax.random.normal(k, shape, F32) * scale

    def gain(k, shape):
        return 1.0 + 0.02 * jax.random.normal(k, shape, F32)

    d_scale = D_MODEL ** -0.5
    return {
        'x_prompt': nrm(ks[0], (BATCH, SEQ, D_MODEL)),
        'x_sample': nrm(ks[1], (DEC_BATCH, DEC_SEQ, D_MODEL)),
        'state_hgrn': nrm(ks[2], (N_A, DEC_BATCH, HG_HEADS, HG_HEAD_DIM, HG_HEAD_DIM), 0.3),
        'cache_k': nrm(ks[3], (n_pool, PAGE_SIZE, FOX_HEADS, FOX_HEAD_DIM)),
        'cache_v': nrm(ks[4], (n_pool, PAGE_SIZE, FOX_HEADS, FOX_HEAD_DIM)),
        'cache_logf': jax.nn.log_sigmoid(FGATE_BIAS_INIT + nrm(ks[5], (n_pool, PAGE_SIZE, FOX_HEADS))),
        'cache_mem_k': nrm(ks[6], (DEPTH, DEC_BATCH, MEM_LEN, MEM_HEADS, MEM_HEAD_DIM)),
        'cache_mem_v': nrm(ks[7], (DEPTH, DEC_BATCH, MEM_LEN, MEM_HEADS, MEM_HEAD_DIM)),
        'page_table': jax.random.permutation(ks[8], n_pool)[:n_used].reshape(DEC_BATCH, n_pages).astype(jnp.int32),
        'mem_prompt': nrm(ks[9], (BATCH, MEM_LEN, D_MODEL)),
        'norm_pre_mix': gain(ks[10], (DEPTH, D_MODEL)),
        'norm_post_mix': gain(ks[11], (DEPTH, D_MODEL)),
        'norm_pre_mlp': gain(ks[12], (DEPTH, D_MODEL)),
        'norm_post_mlp': gain(ks[13], (DEPTH, D_MODEL)),
        'w_in_a': nrm(ks[14], (N_A, D_MODEL, 4 * TOK_WIDTH + MEM_WIDTH), d_scale),
        'hg_lb_logits': nrm(ks[15], (N_A + 1, TOK_WIDTH), 0.5),
        'hg_out_norm': gain(ks[16], (N_A, TOK_WIDTH)),
        'w_in_b': nrm(ks[17], (N_B, D_MODEL, TOK_WIDTH + MEM_WIDTH), d_scale),
        'kv_norm': gain(ks[18], (D_MODEL,)),
        'w_kv_shared': nrm(ks[19], (D_MODEL, 2 * TOK_WIDTH + FOX_HEADS), d_scale),
        'b_fgate': FGATE_BIAS_INIT + nrm(ks[20], (FOX_HEADS,), 0.1),
        'w_out': nrm(ks[21], (DEPTH, MIX_WIDTH, D_MODEL), MIX_WIDTH ** -0.5),
        'mem_norm': gain(ks[22], (D_MODEL,)),
        'w_mem_kv': nrm(ks[23], (DEPTH, D_MODEL, 2 * MEM_WIDTH), d_scale),
        'w_mlp_up': nrm(ks[24], (DEPTH, D_MODEL, D_FF), d_scale),
        'w_mlp_down': nrm(ks[25], (DEPTH, D_FF, D_MODEL), D_FF ** -0.5),
    }


def reference(x_prompt, x_sample, state_hgrn, cache_k, cache_v, cache_logf, cache_mem_k, cache_mem_v,
              page_table, mem_prompt, norm_pre_mix, norm_post_mix, norm_pre_mlp, norm_post_mlp,
              w_in_a, hg_lb_logits, hg_out_norm, w_in_b, kv_norm, w_kv_shared, b_fgate,
              w_out, mem_norm, w_mem_kv, w_mlp_up, w_mlp_down):
    mem_k_prompt, mem_v_prompt = memory_kv(mem_prompt, mem_norm, w_mem_kv)
    hg0 = jnp.zeros((N_A, x_prompt.shape[0], HG_HEADS, HG_HEAD_DIM, HG_HEAD_DIM), F32)
    y_prompt, hg_p, k_prompt, v_prompt, logf_prompt = trunk(
        x_prompt, 0, mem_k_prompt, mem_v_prompt, hg0, None, None, None,
        norm_pre_mix, norm_post_mix, norm_pre_mlp, norm_post_mlp,
        w_in_a, hg_lb_logits, hg_out_norm, w_in_b, kv_norm, w_kv_shared, b_fgate,
        w_out, w_mlp_up, w_mlp_down)
    n_seq, n_pages = page_table.shape
    page = cache_k.shape[1]
    past_len = n_pages * page
    k_past = cache_k[page_table].reshape(n_seq, past_len, FOX_HEADS, FOX_HEAD_DIM)
    v_past = cache_v[page_table].reshape(n_seq, past_len, FOX_HEADS, FOX_HEAD_DIM)
    logf_past = cache_logf[page_table].reshape(n_seq, past_len, FOX_HEADS)
    y_sample, hg_s, k_sample, v_sample, logf_sample = trunk(
        x_sample, past_len, cache_mem_k, cache_mem_v, state_hgrn, k_past, v_past, logf_past,
        norm_pre_mix, norm_post_mix, norm_pre_mlp, norm_post_mlp,
        w_in_a, hg_lb_logits, hg_out_norm, w_in_b, kv_norm, w_kv_shared, b_fgate,
        w_out, w_mlp_up, w_mlp_down)
    hgrn_state_prompt = hg_p.astype(state_hgrn.dtype)
    hgrn_state_sample = hg_s.astype(state_hgrn.dtype)
    return (y_prompt, y_sample, hgrn_state_prompt, hgrn_state_sample,
            k_prompt, v_prompt, logf_prompt, k_sample, v_sample, logf_sample,
            mem_k_prompt, mem_v_prompt)
```

```python
import functools

import jax
import jax.numpy as jnp
from jax import lax
from jax.experimental import pallas as pl
from jax.experimental.pallas import tpu as pltpu

F32 = jnp.float32
BF16 = jnp.bfloat16

NORM_EPS = 1e-6
HEAD_DIM = 128
LANES = 128
SUBLANES = 8
VMEM_LIMIT = 52 * 1024 * 1024
HG_CHUNK = 64
HG_SAFE_DECAY = 60.0
HIGHEST = lax.Precision.HIGHEST

NT_DIMS = (((1,), (1,)), ((), ()))
TN_DIMS = (((0,), (0,)), ((), ()))


def _cparams(semantics):
    return pltpu.CompilerParams(dimension_semantics=semantics, vmem_limit_bytes=VMEM_LIMIT)


def _tile(n, want):
    if n <= want:
        return n
    t = want
    while t >= 16:
        if n % t == 0:
            return t
        t -= 16
    return n


def _rms_scale(x):
    return lax.rsqrt(jnp.mean(x * x, axis=-1, keepdims=True) + NORM_EPS)


def _rmsnorm_body(x_ref, g_ref, o_ref):
    x = x_ref[...]
    o_ref[...] = (x * _rms_scale(x) * g_ref[...]).astype(o_ref.dtype)


def rmsnorm_bf16(x, gain):
    m, d = x.shape
    tm = _tile(m, 256)
    return pl.pallas_call(
        _rmsnorm_body,
        out_shape=jax.ShapeDtypeStruct((m, d), BF16),
        grid=(m // tm,),
        in_specs=[pl.BlockSpec((tm, d), lambda i: (i, 0)), pl.BlockSpec((1, d), lambda i: (0, 0))],
        out_specs=pl.BlockSpec((tm, d), lambda i: (i, 0)),
        compiler_params=_cparams(("parallel",)),
        name="rmsnorm",
    )(x, gain.reshape(1, d))


def _post_body(m_ref, h_ref, gp_ref, *rest, n_next):
    g_next = rest[:n_next]
    ho_ref = rest[n_next]
    hn_refs = rest[n_next + 1:]
    m = m_ref[...]
    h = h_ref[...] + m * _rms_scale(m) * gp_ref[...]
    ho_ref[...] = h
    if n_next:
        hs = h * _rms_scale(h)
        for g_ref, hn_ref in zip(g_next, hn_refs):
            hn_ref[...] = (hs * g_ref[...]).astype(hn_ref.dtype)


def post_norm_residual(mixed, h, gain_post, next_gains):
    m, d = h.shape
    tm = _tile(m, 256)
    n_next = len(next_gains)
    row = pl.BlockSpec((tm, d), lambda i: (i, 0))
    vec = pl.BlockSpec((1, d), lambda i: (0, 0))
    outs = pl.pallas_call(
        functools.partial(_post_body, n_next=n_next),
        out_shape=[jax.ShapeDtypeStruct((m, d), F32)] + [jax.ShapeDtypeStruct((m, d), BF16)] * n_next,
        grid=(m // tm,),
        in_specs=[row, row, vec] + [vec] * n_next,
        out_specs=[row] * (1 + n_next),
        compiler_params=_cparams(("parallel",)),
        name="post_norm_residual",
    )(mixed, h, gain_post.reshape(1, d), *[g.reshape(1, d) for g in next_gains])
    return outs[0], list(outs[1:])


def _epilogue(kind, acc, vecs):
    if kind == "none":
        return acc
    if kind == "silu":
        return acc * jax.nn.sigmoid(acc)
    if kind == "fgate":
        lb = vecs[0][...]
        return lb + (1.0 - lb) * jax.nn.sigmoid(acc)
    if kind == "relu2":
        r = jnp.maximum(acc, 0.0)
        return r * r
    if kind == "logsig":
        return jax.nn.log_sigmoid(acc + vecs[0][...])
    raise ValueError(kind)


def _mm_body(x_ref, w_ref, *rest, kind, n_vec):
    vecs = rest[:n_vec]
    o_ref = rest[n_vec]
    acc = jnp.dot(x_ref[...], w_ref[...], preferred_element_type=F32)
    o_ref[...] = _epilogue(kind, acc, vecs).astype(o_ref.dtype)


def matmul(x, w, col_off, n_cols, kind="none", vecs=(), out_dtype=F32, layers=None):
    m, k = x.shape
    nl = w.shape[0] if layers else 1
    tm = _tile(m, 1024)
    tn = _tile(n_cols, 1024 if m >= 1024 else 512)
    assert col_off % tn == 0 and n_cols % tn == 0
    jo = col_off // tn
    vec_arrs = [v.reshape(1, n_cols).astype(F32) for v in vecs]
    out_shape = jax.ShapeDtypeStruct((nl, m, n_cols), out_dtype)
    return_all = layers
    out = pl.pallas_call(
        functools.partial(_mm_body, kind=kind, n_vec=len(vec_arrs)),
        out_shape=out_shape,
        grid=(nl, m // tm, n_cols // tn),
        in_specs=[pl.BlockSpec((tm, k), lambda l, i, j: (i, 0)),
                  pl.BlockSpec((None, k, tn), lambda l, i, j: (l, 0, jo + j))]
                 + [pl.BlockSpec((1, tn), lambda l, i, j: (0, j))] * len(vec_arrs),
        out_specs=pl.BlockSpec((None, tm, tn), lambda l, i, j: (l, i, j)),
        compiler_params=_cparams(("parallel", "parallel", "parallel")),
        name="matmul_" + kind,
    )(x, w, *vec_arrs)
    return out if return_all else out[0]


def _mm_acc_body(x_ref, w_ref, o_ref, acc_ref):
    kk = pl.program_id(2)

    @pl.when(kk == 0)
    def _():
        acc_ref[...] = jnp.zeros_like(acc_ref)

    acc_ref[...] += jnp.dot(x_ref[...], w_ref[...], preferred_element_type=F32)

    @pl.when(kk == pl.num_programs(2) - 1)
    def _():
        o_ref[...] = acc_ref[...]


def matmul_long_k(x, w):
    m, k = x.shape
    n = w.shape[1]
    tm = _tile(m, 1024)
    tn = _tile(n, 1024)
    tk = _tile(k, 2048)
    return pl.pallas_call(
        _mm_acc_body,
        out_shape=jax.ShapeDtypeStruct((m, n), F32),
        grid=(m // tm, n // tn, k // tk),
        in_specs=[pl.BlockSpec((tm, tk), lambda i, j, kk: (i, kk)),
                  pl.BlockSpec((tk, tn), lambda i, j, kk: (kk, j))],
        out_specs=pl.BlockSpec((tm, tn), lambda i, j, kk: (i, j)),
        scratch_shapes=[pltpu.VMEM((tm, tn), F32)],
        compiler_params=_cparams(("parallel", "parallel", "arbitrary")),
        name="matmul_long_k",
    )(x, w)


def _select_row(group, sel):
    return jnp.sum(jnp.where(sel, group, 0.0), axis=0, keepdims=True)


def _hgrn_body(*refs, n_heads, t_chunk, t_valid, has_s0):
    if has_s0:
        q_ref, f_ref, v_ref, g_ref, gain_ref, s0_ref, o_ref, sout_ref, st_ref, o_scr, lg_scr = refs
    else:
        q_ref, f_ref, v_ref, g_ref, gain_ref, o_ref, sout_ref, st_ref, o_scr, lg_scr = refs
        s0_ref = None
    t_idx = pl.program_id(2)
    d = HEAD_DIM
    c = min(HG_CHUNK, t_chunk)
    n_sub = t_chunk // c

    @pl.when(t_idx == 0)
    def _():
        for h in range(n_heads):
            if has_s0:
                st_ref[h] = s0_ref[0, h].T
            else:
                st_ref[h] = jnp.zeros((d, d), F32)

    row = lax.broadcasted_iota(jnp.int32, (c, c), 0)
    col = lax.broadcasted_iota(jnp.int32, (c, c), 1)
    causal = row >= col
    tri = causal.astype(F32)

    def token_steps(h, first, count):
        hs = pl.ds(h * d, d)

        def step(t, st):
            base = pl.multiple_of(((first + t) // SUBLANES) * SUBLANES, SUBLANES)
            grp = pl.ds(base, SUBLANES)
            sel = lax.broadcasted_iota(jnp.int32, (SUBLANES, d), 0) == (first + t - base)
            f_t = _select_row(f_ref[0, grp, hs], sel)
            v_col = jnp.broadcast_to(_select_row(v_ref[0, grp, hs], sel), (d, d)).T
            st = st * f_t + v_col * (1.0 - f_t)
            q8 = jnp.broadcast_to(_select_row(q_ref[0, grp, hs], sel), (SUBLANES, d))
            o8 = lax.dot_general(q8, st, NT_DIMS, precision=HIGHEST, preferred_element_type=F32)
            o_scr[grp, hs] = jnp.where(sel, o8, o_scr[grp, hs])
            return st

        st_ref[h] = lax.fori_loop(0, count, step, st_ref[h])

    if t_valid is not None:
        assert n_sub == 1 and t_valid <= t_chunk
        o_scr[...] = jnp.zeros_like(o_scr)
        for h in range(n_heads):
            token_steps(h, 0, t_valid)

    for sc in range(n_sub if t_valid is None else 0):
        rows = pl.ds(sc * c, c)
        lg_all = jnp.log(f_ref[0, rows, :])
        lg_scr[...] = lg_all
        total_decay = jnp.min(jnp.sum(lg_all, axis=0, keepdims=True))
        safe = total_decay > -HG_SAFE_DECAY

        @pl.when(safe)
        def _():
            for h in range(n_heads):
                hs = pl.ds(h * d, d)
                lg = lg_scr[:, hs]
                b = jnp.dot(tri, lg, precision=HIGHEST, preferred_element_type=F32)
                q = q_ref[0, rows, hs]
                v = v_ref[0, rows, hs]
                k = 1.0 - f_ref[0, rows, hs]
                qt = (q * jnp.exp(b)).astype(BF16)
                kt = k * jnp.exp(-b)
                e_last = jnp.exp(b[c - 1:c, :])
                kd = (kt * e_last).astype(BF16)
                st = st_ref[h]
                att = lax.dot_general(qt, kt.astype(BF16), NT_DIMS, preferred_element_type=F32)
                att = jnp.where(causal, att, 0.0).astype(BF16)
                vb = v.astype(BF16)
                o = jnp.dot(att, vb, preferred_element_type=F32)
                o = o + lax.dot_general(qt, st.astype(BF16), NT_DIMS, preferred_element_type=F32)
                o_scr[rows, hs] = o
                st_ref[h] = st * e_last + lax.dot_general(vb, kd, TN_DIMS, preferred_element_type=F32)

        @pl.when(jnp.logical_not(safe))
        def _():
            for h in range(n_heads):
                token_steps(h, sc * c, c)

    for h in range(n_heads):
        hs = pl.ds(h * d, d)
        o = o_scr[:, hs]
        o = o * _rms_scale(o) * gain_ref[:, hs]
        o_ref[0, :, hs] = (o * g_ref[0, :, hs]).astype(o_ref.dtype)

    @pl.when(t_idx == pl.num_programs(2) - 1)
    def _():
        for h in range(n_heads):
            sout_ref[0, h] = st_ref[h].T


def hgrn2(q, f, v, g, gain, s0, out_width, t_valid=None):
    bsz, t, w = q.shape
    d = HEAD_DIM
    heads = w // d
    hb = 6 if heads % 6 == 0 else (4 if heads % 4 == 0 else 1)
    tc = _tile(t, 128)
    has_s0 = s0 is not None
    tok = pl.BlockSpec((1, tc, hb * d), lambda b, hg, ti: (b, ti, hg))
    in_specs = [tok, tok, tok, tok, pl.BlockSpec((1, hb * d), lambda b, hg, ti: (0, hg))]
    args = [q, f, v, g, gain.reshape(1, w)]
    state_spec = pl.BlockSpec((1, hb, d, d), lambda b, hg, ti: (b, hg, 0, 0))
    if has_s0:
        in_specs.append(state_spec)
        args.append(s0)
    o, s_out = pl.pallas_call(
        functools.partial(_hgrn_body, n_heads=hb, t_chunk=tc, t_valid=t_valid, has_s0=has_s0),
        out_shape=[jax.ShapeDtypeStruct((bsz, t, out_width), BF16),
                   jax.ShapeDtypeStruct((bsz, heads, d, d), F32)],
        grid=(bsz, heads // hb, t // tc),
        in_specs=in_specs,
        out_specs=[tok, state_spec],
        scratch_shapes=[pltpu.VMEM((hb, d, d), F32),
                        pltpu.VMEM((tc, hb * d), F32),
                        pltpu.VMEM((min(HG_CHUNK, tc), hb * d), F32)],
        compiler_params=_cparams(("parallel", "parallel", "arbitrary")),
        name="hgrn2",
    )(*args)
    return o, s_out


def _memattn_body(q_ref, k_ref, v_ref, buf_ref, o_ref, *, n_heads, head_dim):
    del buf_ref
    scale = head_dim ** -0.5
    for h in range(n_heads):
        hs = pl.ds(h * head_dim, head_dim)
        q = q_ref[0, :, hs].astype(BF16)
        k = k_ref[0, :, hs].astype(BF16)
        s = lax.dot_general(q, k, NT_DIMS, preferred_element_type=F32) * scale
        p = jnp.exp(s - jnp.max(s, axis=-1, keepdims=True))
        p = p / jnp.sum(p, axis=-1, keepdims=True)
        o = jnp.dot(p.astype(BF16), v_ref[0, :, hs].astype(BF16), preferred_element_type=F32)
        o_ref[0, :, hs] = o.astype(o_ref.dtype)


def memory_attention(qm, mem_k, mem_v, layer, n_heads, o_buf, col_off):
    bsz, t, w = qm.shape
    m = mem_k.shape[2]
    tq = _tile(t, 512)
    assert col_off % w == 0
    cb = col_off // w
    kv_spec = pl.BlockSpec((None, 1, m, w), lambda b, i: (layer, b, 0, 0))
    return pl.pallas_call(
        functools.partial(_memattn_body, n_heads=n_heads, head_dim=w // n_heads),
        out_shape=jax.ShapeDtypeStruct(o_buf.shape, o_buf.dtype),
        grid=(bsz, t // tq),
        in_specs=[pl.BlockSpec((1, tq, w), lambda b, i: (b, i, 0)), kv_spec, kv_spec,
                  pl.BlockSpec(memory_space=pl.ANY)],
        out_specs=pl.BlockSpec((1, tq, w), lambda b, i: (b, i, cb)),
        input_output_aliases={3: 0},
        compiler_params=_cparams(("parallel", "parallel")),
        name="memory_attention",
    )(qm, mem_k, mem_v, o_buf)


def _cumsum_body(x_ref, c_ref, ct_ref, carry_ref):
    @pl.when(pl.program_id(1) == 0)
    def _():
        carry_ref[...] = jnp.zeros_like(carry_ref)

    n = x_ref.shape[1]
    tri = (lax.broadcasted_iota(jnp.int32, (n, n), 0) >= lax.broadcasted_iota(jnp.int32, (n, n), 1)).astype(F32)
    c = jnp.dot(tri, x_ref[0], precision=HIGHEST, preferred_element_type=F32) + carry_ref[...]
    carry_ref[...] = c[n - 1:n, :]
    c_ref[0] = c
    ct_ref[0] = c.T


def cumsum_tokens(x):
    bsz, t, w = x.shape
    tc = _tile(t, 256)
    return pl.pallas_call(
        _cumsum_body,
        out_shape=[jax.ShapeDtypeStruct((bsz, t, w), F32), jax.ShapeDtypeStruct((bsz, w, t), F32)],
        grid=(bsz, t // tc),
        in_specs=[pl.BlockSpec((1, tc, w), lambda b, i: (b, i, 0))],
        out_specs=[pl.BlockSpec((1, tc, w), lambda b, i: (b, i, 0)),
                   pl.BlockSpec((1, w, tc), lambda b, i: (b, 0, i))],
        scratch_shapes=[pltpu.VMEM((1, w), F32)],
        compiler_params=_cparams(("parallel", "arbitrary")),
        name="cumsum_tokens",
    )(x)


def _fox_prompt_body(q_ref, k_ref, v_ref, c_ref, ct_ref, o_ref, kb_ref, vb_ref, m_ref, l_ref, acc_ref, *, tq):
    h = pl.program_id(1)
    i = pl.program_id(2)
    scale = HEAD_DIM ** -0.5

    @pl.when(i == 0)
    def _():
        kb_ref[...] = k_ref[0].astype(BF16)
        vb_ref[...] = v_ref[0].astype(BF16)

    q = q_ref[0].astype(BF16)
    lane = lax.broadcasted_iota(jnp.int32, (tq, LANES), 1)
    cq = jnp.sum(jnp.where(lane == h, c_ref[0], 0.0), axis=1, keepdims=True)
    h_base = pl.multiple_of((h // SUBLANES) * SUBLANES, SUBLANES)
    h_grp = pl.ds(h_base, SUBLANES)
    h_sel = lax.broadcasted_iota(jnp.int32, (SUBLANES, tq), 0) == (h - h_base)
    m_ref[...] = jnp.full_like(m_ref, -jnp.inf)
    l_ref[...] = jnp.zeros_like(l_ref)
    acc_ref[...] = jnp.zeros_like(acc_ref)

    def tile(j, masked):
        ks = pl.ds(pl.multiple_of(j * tq, tq), tq)
        s = lax.dot_general(q, kb_ref[ks, :], NT_DIMS, preferred_element_type=F32) * scale
        s = s + cq - _select_row(ct_ref[0, h_grp, ks], h_sel)
        if masked:
            r = lax.broadcasted_iota(jnp.int32, (tq, tq), 0)
            cc = lax.broadcasted_iota(jnp.int32, (tq, tq), 1)
            s = jnp.where(cc <= r, s, -jnp.inf)
        m_old = m_ref[...]
        m_new = jnp.maximum(m_old, jnp.max(s, axis=-1, keepdims=True))
        alpha = jnp.exp(m_old - m_new)
        p = jnp.exp(s - m_new)
        l_ref[...] = alpha * l_ref[...] + jnp.sum(p, axis=-1, keepdims=True)
        acc_ref[...] = alpha * acc_ref[...] + jnp.dot(p.astype(BF16), vb_ref[ks, :], preferred_element_type=F32)
        m_ref[...] = m_new

    def body(j, carry):
        tile(j, False)
        return carry

    lax.fori_loop(0, i, body, 0)
    tile(i, True)
    o_ref[0] = (acc_ref[...] / l_ref[...]).astype(o_ref.dtype)


def fox_attention_prompt(q, k, v, c, ct, out_width):
    bsz, t, w = q.shape
    d = HEAD_DIM
    heads = w // d
    tq = _tile(t, 256)
    return pl.pallas_call(
        functools.partial(_fox_prompt_body, tq=tq),
        out_shape=jax.ShapeDtypeStruct((bsz, t, out_width), BF16),
        grid=(bsz, heads, t // tq),
        in_specs=[pl.BlockSpec((1, tq, d), lambda b, h, i: (b, i, h)),
                  pl.BlockSpec((1, t, d), lambda b, h, i: (b, 0, h)),
                  pl.BlockSpec((1, t, d), lambda b, h, i: (b, 0, h)),
                  pl.BlockSpec((1, tq, LANES), lambda b, h, i: (b, i, 0)),
                  pl.BlockSpec((1, LANES, t), lambda b, h, i: (b, 0, 0))],
        out_specs=pl.BlockSpec((1, tq, d), lambda b, h, i: (b, i, h)),
        scratch_shapes=[pltpu.VMEM((t, d), BF16), pltpu.VMEM((t, d), BF16),
                        pltpu.VMEM((tq, 1), F32), pltpu.VMEM((tq, 1), F32), pltpu.VMEM((tq, d), F32)],
        compiler_params=_cparams(("parallel", "parallel", "arbitrary")),
        name="fox_attention_prompt",
    )(q, k, v, c, ct)


def _suffix_body(pt_ref, x_ref, d_ref, carry_ref):
    del pt_ref

    @pl.when(pl.program_id(1) == 0)
    def _():
        carry_ref[...] = jnp.zeros_like(carry_ref)

    x = x_ref[0]
    n = x.shape[1]
    later = (lax.broadcasted_iota(jnp.int32, (n, n), 0) > lax.broadcasted_iota(jnp.int32, (n, n), 1)).astype(F32)
    d_ref[0, 0] = jnp.dot(x, later, precision=HIGHEST, preferred_element_type=F32) + carry_ref[...]
    carry_ref[...] = carry_ref[...] + jnp.sum(x, axis=1, keepdims=True)


def suffix_log_forget(logf_t, page_table):
    bsz, n_pages = page_table.shape
    _, heads, page = logf_t.shape
    grid_spec = pltpu.PrefetchScalarGridSpec(
        num_scalar_prefetch=1,
        grid=(bsz, n_pages),
        in_specs=[pl.BlockSpec((1, heads, page), lambda b, p, pt: (pt[b, n_pages - 1 - p], 0, 0))],
        out_specs=pl.BlockSpec((1, 1, heads, page), lambda b, p, pt: (b, n_pages - 1 - p, 0, 0)),
        scratch_shapes=[pltpu.VMEM((heads, page), F32)],
    )
    return pl.pallas_call(
        _suffix_body,
        out_shape=jax.ShapeDtypeStruct((bsz, n_pages, heads, page), F32),
        grid_spec=grid_spec,
        compiler_params=_cparams(("parallel", "arbitrary")),
        name="suffix_log_forget",
    )(page_table, logf_t)


def _fox_decode_body(pt_ref, q_ref, *rest, n_heads, pages_per_step):
    del pt_ref
    r_pages = pages_per_step
    k_refs = rest[:r_pages]
    v_refs = rest[r_pages:2 * r_pages]
    (d_ref, cn_ref, cnt_ref, kn_ref, vn_ref, o_ref, m_ref, l_ref, acc_ref) = rest[2 * r_pages:]
    p_idx = pl.program_id(1)
    d = HEAD_DIM
    tq = q_ref.shape[1]
    scale = d ** -0.5

    @pl.when(p_idx == 0)
    def _():
        m_ref[...] = jnp.full_like(m_ref, -jnp.inf)
        l_ref[...] = jnp.zeros_like(l_ref)
        acc_ref[...] = jnp.zeros_like(acc_ref)

    def update(h, s, v_tiles):
        hs = pl.ds(h * d, d)
        m_old = m_ref[:, hs]
        m_new = jnp.maximum(m_old, jnp.max(s, axis=-1, keepdims=True))
        alpha = jnp.exp(m_old - m_new)
        p = jnp.exp(s - m_new[:, 0:1])
        l_ref[:, hs] = alpha * l_ref[:, hs] + jnp.sum(p, axis=-1, keepdims=True)
        p = p.astype(BF16)
        pv = jnp.dot(p[:, 0:d], v_tiles[0], preferred_element_type=F32)
        for r in range(1, len(v_tiles)):
            pv = pv + jnp.dot(p[:, r * d:(r + 1) * d], v_tiles[r], preferred_element_type=F32)
        acc_ref[:, hs] = alpha * acc_ref[:, hs] + pv
        m_ref[:, hs] = m_new

    for h in range(n_heads):
        hs = pl.ds(h * d, d)
        q = q_ref[0, :, hs].astype(BF16)
        cq = cn_ref[0, :, h:h + 1]
        s_parts = []
        for r in range(r_pages):
            s = lax.dot_general(q, k_refs[r][0, :, hs].astype(BF16), NT_DIMS, preferred_element_type=F32) * scale
            s_parts.append(s + cq + d_ref[0, r, h:h + 1, :])
        s = s_parts[0] if r_pages == 1 else jnp.concatenate(s_parts, axis=1)
        update(h, s, [v_refs[r][0, :, hs].astype(BF16) for r in range(r_pages)])

    @pl.when(p_idx == pl.num_programs(1) - 1)
    def _():
        n_new = kn_ref.shape[1]
        r_i = lax.broadcasted_iota(jnp.int32, (tq, n_new), 0)
        c_i = lax.broadcasted_iota(jnp.int32, (tq, n_new), 1)
        for h in range(n_heads):
            hs = pl.ds(h * d, d)
            q = q_ref[0, :, hs].astype(BF16)
            s = lax.dot_general(q, kn_ref[0, :, hs].astype(BF16), NT_DIMS, preferred_element_type=F32) * scale
            s = s + cn_ref[0, :, h:h + 1] - cnt_ref[0, h:h + 1, :]
            s = jnp.where(c_i <= r_i, s, -jnp.inf)
            update(h, s, [vn_ref[0, :, hs].astype(BF16)])
        for h in range(n_heads):
            hs = pl.ds(h * d, d)
            o_ref[0, :, hs] = (acc_ref[:, hs] / l_ref[:, hs]).astype(o_ref.dtype)


def fox_attention_decode(q, cache_k, cache_v, d_suffix, cn, cnt, k_new, v_new, page_table, out_width):
    bsz, tq, w = q.shape
    _, page, _ = cache_k.shape
    n_pages = page_table.shape[1]
    heads = w // HEAD_DIM
    tn = k_new.shape[1]
    r_pages = 4 if n_pages % 4 == 0 else 1
    steps = n_pages // r_pages

    def page_spec(r):
        return pl.BlockSpec((1, page, w), lambda b, p, pt: (pt[b, p * r_pages + r], 0, 0))

    whole = lambda shape: pl.BlockSpec((1,) + shape, lambda b, p, pt: (b, 0, 0))
    grid_spec = pltpu.PrefetchScalarGridSpec(
        num_scalar_prefetch=1,
        grid=(bsz, steps),
        in_specs=[whole((tq, w))] + [page_spec(r) for r in range(r_pages)] * 2
                 + [pl.BlockSpec((1, r_pages, heads, page), lambda b, p, pt: (b, p, 0, 0)),
                    pl.BlockSpec((1, tq, LANES), lambda b, p, pt: (b, 0, 0)),
                    whole((LANES, tn)), whole((tn, w)), whole((tn, w))],
        out_specs=pl.BlockSpec((1, tq, w), lambda b, p, pt: (b, 0, 0)),
        scratch_shapes=[pltpu.VMEM((tq, w), F32), pltpu.VMEM((tq, w), F32), pltpu.VMEM((tq, w), F32)],
    )
    return pl.pallas_call(
        functools.partial(_fox_decode_body, n_heads=heads, pages_per_step=r_pages),
        out_shape=jax.ShapeDtypeStruct((bsz, tq, out_width), BF16),
        grid_spec=grid_spec,
        compiler_params=_cparams(("parallel", "arbitrary")),
        name="fox_attention_decode",
    )(page_table, q, *([cache_k] * r_pages), *([cache_v] * r_pages), d_suffix, cn, cnt, k_new, v_new)


def _pad_rows(x, rows):
    pad = [(0, 0)] * x.ndim
    pad[1] = (0, rows - x.shape[1])
    return jnp.pad(x, pad)


def _trunk(h, bsz, t, t_valid, mem_k, mem_v, hg_s0, decode, wts):
    d_model = h.shape[1]
    tok_w = wts["tok_w"]
    mem_w = d_model - tok_w
    mem_heads = mem_k.shape[3]
    mk = mem_k.reshape(mem_k.shape[:3] + (mem_w,))
    mv = mem_v.reshape(mem_v.shape[:3] + (mem_w,))
    n_a = wts["w_in_a"].shape[0]
    depth = wts["w_out"].shape[0]
    states = []
    hn = rmsnorm_bf16(h, wts["norm_pre_mix"][0])
    k_new = v_new = logf_new = None
    hkv = None
    for layer in range(depth):
        if layer == n_a:
            wkv = wts["w_kv"]
            k_new = matmul(hkv, wkv, 0, tok_w)
            v_new = matmul(hkv, wkv, tok_w, tok_w)
            logf_new = matmul(hkv, wts["w_fg"], 0, LANES, kind="logsig", vecs=(wts["b_fg"],))
        if layer < n_a:
            w_in = wts["w_in_a"][layer:layer + 1]
            q = matmul(hn, w_in, 0, tok_w, kind="silu")
            f = matmul(hn, w_in, tok_w, tok_w, kind="fgate", vecs=(wts["lb"][layer],))
            v = matmul(hn, w_in, 2 * tok_w, tok_w)
            g = matmul(hn, w_in, 3 * tok_w, tok_w, kind="silu")
            zm = matmul(hn, w_in, 4 * tok_w, mem_w)
            shp = (bsz, t, tok_w)
            s0 = None if hg_s0 is None else hg_s0[layer]
            o_buf, s_fin = hgrn2(q.reshape(shp), f.reshape(shp), v.reshape(shp), g.reshape(shp),
                                 wts["hg_out_norm"][layer], s0, d_model, t_valid)
            states.append(s_fin)
        else:
            w_in = wts["w_in_b"][layer - n_a:layer - n_a + 1]
            q = matmul(hn, w_in, 0, tok_w).reshape(bsz, t, tok_w)
            zm = matmul(hn, w_in, tok_w, mem_w)
            if decode is None:
                c, ct = cumsum_tokens(logf_new.reshape(bsz, t, LANES))
                o_buf = fox_attention_prompt(q, k_new.reshape(bsz, t, tok_w), v_new.reshape(bsz, t, tok_w),
                                             c, ct, d_model)
            else:
                cn, cnt = cumsum_tokens(_pad_rows(logf_new.reshape(bsz, t, LANES), LANES))
                kn = _pad_rows(k_new.reshape(bsz, t, tok_w), LANES)
                vn = _pad_rows(v_new.reshape(bsz, t, tok_w), LANES)
                o_buf = fox_attention_decode(q, decode["cache_k"], decode["cache_v"], decode["d_suffix"],
                                             cn, cnt, kn, vn, decode["page_table"], d_model)
        o_buf = memory_attention(zm.reshape(bsz, t, mem_w), mk, mv, layer, mem_heads, o_buf, tok_w)
        mixed = matmul(o_buf.reshape(bsz * t, d_model), wts["w_out"][layer:layer + 1], 0, d_model)
        h, (hn,) = post_norm_residual(mixed, h, wts["norm_post_mix"][layer], [wts["norm_pre_mlp"][layer]])
        u = matmul(hn, wts["w_mlp_up"][layer:layer + 1], 0, wts["w_mlp_up"].shape[2], kind="relu2", out_dtype=BF16)
        down = matmul_long_k(u, wts["w_mlp_down"][layer])
        if layer + 1 < depth:
            nxt = [wts["norm_pre_mix"][layer + 1]] + ([wts["kv_norm"]] if layer + 1 == n_a else [])
            h, hns = post_norm_residual(down, h, wts["norm_post_mlp"][layer], nxt)
            hn = hns[0]
            hkv = hns[1] if layer + 1 == n_a else None
        else:
            h, _ = post_norm_residual(down, h, wts["norm_post_mlp"][layer], [])
    return h, jnp.stack(states), k_new, v_new, logf_new


def kernel(x_prompt, x_sample, state_hgrn, cache_k, cache_v, cache_logf, cache_mem_k, cache_mem_v, page_table, mem_prompt, norm_pre_mix, norm_post_mix, norm_pre_mlp, norm_post_mlp, w_in_a, hg_lb_logits, hg_out_norm, w_in_b, kv_norm, w_kv_shared, b_fgate, w_out, mem_norm, w_mem_kv, w_mlp_up, w_mlp_down):
    bp, tp, d_model = x_prompt.shape
    bs, ts, _ = x_sample.shape
    fox_heads, fox_dim = cache_k.shape[2], cache_k.shape[3]
    assert fox_dim == HEAD_DIM and state_hgrn.shape[-1] == HEAD_DIM
    tok_w = fox_heads * fox_dim
    mem_w = d_model - tok_w
    mem_len = mem_prompt.shape[1]
    depth = w_out.shape[0]
    n_pool, page = cache_k.shape[0], cache_k.shape[1]

    lb_all = jnp.cumsum(jax.nn.softmax(hg_lb_logits.astype(F32), axis=0), axis=0)
    wts = {
        "tok_w": tok_w,
        "norm_pre_mix": norm_pre_mix, "norm_post_mix": norm_post_mix,
        "norm_pre_mlp": norm_pre_mlp, "norm_post_mlp": norm_post_mlp,
        "kv_norm": kv_norm, "hg_out_norm": hg_out_norm, "lb": lb_all,
        "w_in_a": w_in_a.astype(BF16), "w_in_b": w_in_b.astype(BF16),
        "w_kv": w_kv_shared.astype(BF16)[None],
        "w_fg": jnp.pad(w_kv_shared[:, 2 * tok_w:], ((0, 0), (0, LANES - fox_heads))).astype(BF16)[None],
        "b_fg": jnp.pad(b_fgate.astype(F32), (0, LANES - fox_heads)),
        "w_out": w_out.astype(BF16), "w_mlp_up": w_mlp_up.astype(BF16), "w_mlp_down": w_mlp_down.astype(BF16),
    }

    mem_n = rmsnorm_bf16(mem_prompt.reshape(bp * mem_len, d_model), mem_norm)
    w_mem = w_mem_kv.astype(BF16)
    mem_k_flat = matmul(mem_n, w_mem, 0, mem_w, layers=True)
    mem_v_flat = matmul(mem_n, w_mem, mem_w, mem_w, layers=True)
    mem_heads = cache_mem_k.shape[3]
    mem_shape = (depth, bp, mem_len, mem_heads, mem_w // mem_heads)
    mem_k_prompt = mem_k_flat.reshape(mem_shape)
    mem_v_prompt = mem_v_flat.reshape(mem_shape)
    y_p, hg_p, k_p, v_p, lf_p = _trunk(x_prompt.reshape(bp * tp, d_model), bp, tp, None,
                                       mem_k_prompt, mem_v_prompt, None, None, wts)

    t_pad = max(SUBLANES, ts)
    xs = _pad_rows(x_sample, t_pad).reshape(bs * t_pad, d_model)
    logf_t = cache_logf.transpose(0, 2, 1)
    decode = {
        "cache_k": cache_k.reshape(n_pool, page, tok_w),
        "cache_v": cache_v.reshape(n_pool, page, tok_w),
        "d_suffix": suffix_log_forget(logf_t, page_table),
        "page_table": page_table,
    }
    y_s, hg_s, k_s, v_s, lf_s = _trunk(xs, bs, t_pad, ts, cache_mem_k, cache_mem_v, state_hgrn, decode, wts)

    def unpad(a, width):
        return a.reshape(bs, t_pad, width)[:, :ts]

    return (y_p.reshape(bp, tp, d_model),
            unpad(y_s, d_model),
            hg_p.astype(state_hgrn.dtype),
            hg_s.astype(state_hgrn.dtype),
            k_p.reshape(bp, tp, fox_heads, fox_dim),
            v_p.reshape(bp, tp, fox_heads, fox_dim),
            lf_p[:, :fox_heads].reshape(bp, tp, fox_heads),
            unpad(k_s, tok_w).reshape(bs, ts, fox_heads, fox_dim),
            unpad(v_s, tok_w).reshape(bs, ts, fox_heads, fox_dim),
            unpad(lf_s, LANES)[:, :, :fox_heads],
            mem_k_prompt, mem_v_prompt)
```

```python
import functools

import jax
import jax.numpy as jnp
from jax import lax
from jax.experimental import pallas as pl
from jax.experimental.pallas import tpu as pltpu

F32 = jnp.float32
BF16 = jnp.bfloat16

NORM_EPS = 1e-6
HEAD_DIM = 128
LANES = 128
SUBLANES = 8
VMEM_LIMIT = 52 * 1024 * 1024
HG_CHUNK = 64
HG_SAFE_DECAY = 60.0
HIGHEST = lax.Precision.HIGHEST

NT_DIMS = (((1,), (1,)), ((), ()))
TN_DIMS = (((0,), (0,)), ((), ()))


def _cparams(semantics):
    return pltpu.CompilerParams(dimension_semantics=semantics, vmem_limit_bytes=VMEM_LIMIT)


def _tile(n, want):
    if n <= want:
        return n
    t = want
    while t >= 16:
        if n % t == 0:
            return t
        t -= 16
    return n


def _lane_tiles(x, n):
    return x if n == 1 else jnp.concatenate([x] * n, axis=1)


def _select_row(group, sel):
    return jnp.sum(jnp.where(sel, group, 0.0), axis=0, keepdims=True)


def _rms_scale(x):
    return lax.rsqrt(jnp.mean(x * x, axis=-1, keepdims=True) + NORM_EPS)


def _rmsnorm_body(x_ref, g_ref, o_ref):
    x = x_ref[...]
    o_ref[...] = (x * _rms_scale(x) * g_ref[...]).astype(o_ref.dtype)


def rmsnorm_bf16(x, gain):
    m, d = x.shape
    tm = _tile(m, 256)
    return pl.pallas_call(
        _rmsnorm_body,
        out_shape=jax.ShapeDtypeStruct((m, d), BF16),
        grid=(m // tm,),
        in_specs=[pl.BlockSpec((tm, d), lambda i: (i, 0)), pl.BlockSpec((1, d), lambda i: (0, 0))],
        out_specs=pl.BlockSpec((tm, d), lambda i: (i, 0)),
        compiler_params=_cparams(("parallel",)),
        name="rmsnorm",
    )(x, gain.reshape(1, d))


def _post_body(m_ref, h_ref, gp_ref, *rest, n_next):
    g_next = rest[:n_next]
    ho_ref = rest[n_next]
    hn_refs = rest[n_next + 1:]
    m = m_ref[...]
    h = h_ref[...] + m * _rms_scale(m) * gp_ref[...]
    ho_ref[...] = h
    if n_next:
        hs = h * _rms_scale(h)
        for g_ref, hn_ref in zip(g_next, hn_refs):
            hn_ref[...] = (hs * g_ref[...]).astype(hn_ref.dtype)


def post_norm_residual(mixed, h, gain_post, next_gains):
    m, d = h.shape
    tm = _tile(m, 256)
    n_next = len(next_gains)
    row = pl.BlockSpec((tm, d), lambda i: (i, 0))
    vec = pl.BlockSpec((1, d), lambda i: (0, 0))
    outs = pl.pallas_call(
        functools.partial(_post_body, n_next=n_next),
        out_shape=[jax.ShapeDtypeStruct((m, d), F32)] + [jax.ShapeDtypeStruct((m, d), BF16)] * n_next,
        grid=(m // tm,),
        in_specs=[row, row, vec] + [vec] * n_next,
        out_specs=[row] * (1 + n_next),
        compiler_params=_cparams(("parallel",)),
        name="post_norm_residual",
    )(mixed, h, gain_post.reshape(1, d), *[g.reshape(1, d) for g in next_gains])
    return outs[0], list(outs[1:])


def _epilogue(kind, acc, vecs):
    if kind == "none":
        return acc
    if kind == "silu":
        return acc * jax.nn.sigmoid(acc)
    if kind == "fgate":
        lb = vecs[0][...]
        return lb + (1.0 - lb) * jax.nn.sigmoid(acc)
    if kind == "relu2":
        r = jnp.maximum(acc, 0.0)
        return r * r
    if kind == "logsig":
        return jax.nn.log_sigmoid(acc + vecs[0][...])
    raise ValueError(kind)


def _mm_body(*refs, kind, n_vec, has_second):
    if has_second:
        x_ref, x2_ref, w_ref = refs[:3]
        rest = refs[3:]
    else:
        x_ref, w_ref = refs[:2]
        x2_ref = None
        rest = refs[2:]
    vecs = rest[:n_vec]
    outs = rest[n_vec:]
    wb_ref = outs[-1]
    o_ref = outs[0]
    i = pl.program_id(2)

    @pl.when(i == 0)
    def _():
        wb_ref[...] = w_ref[...].astype(BF16)

    acc = jnp.dot(x_ref[...], wb_ref[...], preferred_element_type=F32)
    o_ref[...] = _epilogue(kind, acc, vecs).astype(o_ref.dtype)

    if has_second:
        o2_ref = outs[1]

        @pl.when(i == pl.num_programs(2) - 1)
        def _():
            acc2 = jnp.dot(x2_ref[...], wb_ref[...], preferred_element_type=F32)
            o2_ref[...] = _epilogue(kind, acc2, vecs).astype(o2_ref.dtype)


def matmul(x, x2, w, col_off, n_cols, kind="none", vecs=(), out_dtype=F32, layer=0, all_layers=False):
    m, k = x.shape
    nl = w.shape[0] if all_layers else 1
    tm = _tile(m, 1024)
    tn = _tile(n_cols, 512)
    assert col_off % tn == 0 and n_cols % tn == 0
    jo = col_off // tn
    has_second = x2 is not None
    vec_arrs = [v.reshape(1, n_cols).astype(F32) for v in vecs]
    in_specs = [pl.BlockSpec((tm, k), lambda l, j, i: (i, 0))]
    args = [x]
    out_shape = [jax.ShapeDtypeStruct((nl, m, n_cols), out_dtype)]
    out_specs = [pl.BlockSpec((None, tm, tn), lambda l, j, i: (l, i, j))]
    if has_second:
        m2 = x2.shape[0]
        in_specs.append(pl.BlockSpec((m2, k), lambda l, j, i: (0, 0)))
        args.append(x2)
        out_shape.append(jax.ShapeDtypeStruct((nl, m2, n_cols), out_dtype))
        out_specs.append(pl.BlockSpec((None, m2, tn), lambda l, j, i: (l, 0, j)))
    in_specs.append(pl.BlockSpec((None, k, tn), lambda l, j, i: (l if all_layers else layer, 0, jo + j)))
    in_specs += [pl.BlockSpec((1, tn), lambda l, j, i: (0, j))] * len(vec_arrs)
    outs = pl.pallas_call(
        functools.partial(_mm_body, kind=kind, n_vec=len(vec_arrs), has_second=has_second),
        out_shape=out_shape,
        grid=(nl, n_cols // tn, m // tm),
        in_specs=in_specs,
        out_specs=out_specs,
        scratch_shapes=[pltpu.VMEM((k, tn), BF16)],
        compiler_params=_cparams(("parallel", "parallel", "arbitrary")),
        name="matmul_" + kind,
    )(*args, w, *vec_arrs)
    if all_layers:
        return outs[0]
    return (outs[0][0], outs[1][0]) if has_second else outs[0][0]


def _mm_acc_body(x_ref, x2_ref, w_ref, o_ref, o2_ref, acc_ref, acc2_ref):
    i = pl.program_id(1)
    kk = pl.program_id(2)
    last = pl.num_programs(2) - 1

    @pl.when(kk == 0)
    def _():
        acc_ref[...] = jnp.zeros_like(acc_ref)

    acc_ref[...] += jnp.dot(x_ref[...], w_ref[...], preferred_element_type=F32)

    @pl.when(kk == last)
    def _():
        o_ref[...] = acc_ref[...]

    first = i == 0

    @pl.when(jnp.logical_and(first, kk == 0))
    def _():
        acc2_ref[...] = jnp.zeros_like(acc2_ref)

    @pl.when(first)
    def _():
        acc2_ref[...] += jnp.dot(x2_ref[...], w_ref[...], preferred_element_type=F32)

    @pl.when(jnp.logical_and(first, kk == last))
    def _():
        o2_ref[...] = acc2_ref[...]


def matmul_long_k(x, x2, w, layer):
    m, k = x.shape
    m2 = x2.shape[0]
    n = w.shape[2]
    tm = _tile(m, 1024)
    tn = _tile(n, 1024)
    tk = _tile(k, 2048)
    return pl.pallas_call(
        _mm_acc_body,
        out_shape=[jax.ShapeDtypeStruct((m, n), F32), jax.ShapeDtypeStruct((m2, n), F32)],
        grid=(n // tn, m // tm, k // tk),
        in_specs=[pl.BlockSpec((tm, tk), lambda j, i, kk: (i, kk)),
                  pl.BlockSpec((m2, tk), lambda j, i, kk: (0, kk)),
                  pl.BlockSpec((None, tk, tn), lambda j, i, kk: (layer, kk, j))],
        out_specs=[pl.BlockSpec((tm, tn), lambda j, i, kk: (i, j)),
                   pl.BlockSpec((m2, tn), lambda j, i, kk: (0, j))],
        scratch_shapes=[pltpu.VMEM((tm, tn), F32), pltpu.VMEM((m2, tn), F32)],
        compiler_params=_cparams(("parallel", "arbitrary", "arbitrary")),
        name="matmul_long_k",
    )(x, x2, w)


def _hgrn_body(*refs, n_heads, t_chunk, t_valid, has_s0):
    if has_s0:
        q_ref, f_ref, v_ref, g_ref, gain_ref, s0_ref, o_ref, sout_ref, st_ref, o_scr = refs
    else:
        q_ref, f_ref, v_ref, g_ref, gain_ref, o_ref, sout_ref, st_ref, o_scr = refs
        s0_ref = None
    t_idx = pl.program_id(2)
    d = HEAD_DIM
    c = min(HG_CHUNK, t_chunk)
    n_sub = t_chunk // c
    heads = range(n_heads)
    lanes = [slice(h * d, (h + 1) * d) for h in heads]

    @pl.when(t_idx == 0)
    def _():
        for h in heads:
            if has_s0:
                st_ref[h] = s0_ref[0, h].T
            else:
                st_ref[h] = jnp.zeros((d, d), F32)

    if t_valid is not None:
        assert n_sub == 1 and t_valid <= t_chunk
        o_scr[...] = jnp.zeros_like(o_scr)
        sts = [st_ref[h] for h in heads]
        for t in range(t_valid):
            f_t = f_ref[0, t:t + 1, :]
            v_t = v_ref[0, t:t + 1, :]
            q_t = q_ref[0, t:t + 1, :]
            v_cols = [jnp.broadcast_to(v_t[:, lanes[h]], (d, d)).T for h in heads]
            sts = [sts[h] * f_t[:, lanes[h]] + v_cols[h] * (1.0 - f_t[:, lanes[h]]) for h in heads]
            q8s = [jnp.broadcast_to(q_t[:, lanes[h]], (SUBLANES, d)).astype(BF16) for h in heads]
            o8s = [lax.dot_general(q8s[h], sts[h].astype(BF16), NT_DIMS, preferred_element_type=F32) for h in heads]
            for h in heads:
                o_scr[t:t + 1, lanes[h]] = o8s[h][0:1, :]
        for h in heads:
            st_ref[h] = sts[h]

    def token_steps(h, first, count):
        hs = pl.ds(h * d, d)

        def step(t, st):
            base = pl.multiple_of(((first + t) // SUBLANES) * SUBLANES, SUBLANES)
            grp = pl.ds(base, SUBLANES)
            sel = lax.broadcasted_iota(jnp.int32, (SUBLANES, d), 0) == (first + t - base)
            f_t = _select_row(f_ref[0, grp, hs], sel)
            v_col = jnp.broadcast_to(_select_row(v_ref[0, grp, hs], sel), (d, d)).T
            st = st * f_t + v_col * (1.0 - f_t)
            q8 = jnp.broadcast_to(_select_row(q_ref[0, grp, hs], sel), (SUBLANES, d)).astype(BF16)
            o8 = lax.dot_general(q8, st.astype(BF16), NT_DIMS, preferred_element_type=F32)
            o_scr[grp, hs] = jnp.where(sel, o8, o_scr[grp, hs])
            return st

        st_ref[h] = lax.fori_loop(0, count, step, st_ref[h])

    row = lax.broadcasted_iota(jnp.int32, (c, c), 0)
    col = lax.broadcasted_iota(jnp.int32, (c, c), 1)
    causal = row >= col
    tri = causal.astype(F32)

    for sc in range(n_sub if t_valid is None else 0):
        rows = pl.ds(sc * c, c)
        f_all = f_ref[0, rows, :]
        lg_all = jnp.log(f_all)
        total_decay = jnp.min(jnp.sum(lg_all, axis=0, keepdims=True))
        safe = total_decay > -HG_SAFE_DECAY

        @pl.when(safe)
        def _():
            b = jnp.dot(tri, jnp.log(f_ref[0, rows, :]), precision=HIGHEST, preferred_element_type=F32)
            e_last = jnp.exp(b[c - 1:c, :])
            qt = (q_ref[0, rows, :] * jnp.exp(b)).astype(BF16)
            kt = (1.0 - f_ref[0, rows, :]) * jnp.exp(-b)
            kd = (kt * e_last).astype(BF16)
            kt = kt.astype(BF16)
            vb = v_ref[0, rows, :].astype(BF16)
            stb = [st_ref[h].astype(BF16) for h in heads]
            att = [lax.dot_general(qt[:, lanes[h]], kt[:, lanes[h]], NT_DIMS, preferred_element_type=F32)
                   for h in heads]
            o_inter = [lax.dot_general(qt[:, lanes[h]], stb[h], NT_DIMS, preferred_element_type=F32) for h in heads]
            upd = [lax.dot_general(vb[:, lanes[h]], kd[:, lanes[h]], TN_DIMS, preferred_element_type=F32)
                   for h in heads]
            att = [jnp.where(causal, a, 0.0).astype(BF16) for a in att]
            o_intra = [jnp.dot(att[h], vb[:, lanes[h]], preferred_element_type=F32) for h in heads]
            for h in heads:
                o_scr[rows, lanes[h]] = o_intra[h] + o_inter[h]
                st_ref[h] = st_ref[h] * e_last[:, lanes[h]] + upd[h]

        @pl.when(jnp.logical_not(safe))
        def _():
            for h in heads:
                token_steps(h, sc * c, c)

    for h in heads:
        o = o_scr[:, lanes[h]]
        o = o * _rms_scale(o) * gain_ref[:, lanes[h]]
        o_ref[0, :, lanes[h]] = (o * g_ref[0, :, lanes[h]]).astype(o_ref.dtype)

    @pl.when(t_idx == pl.num_programs(2) - 1)
    def _():
        for h in heads:
            sout_ref[0, h] = st_ref[h].T


def hgrn2(q, f, v, g, gain, s0, out_width, t_valid=None):
    bsz, t, w = q.shape
    d = HEAD_DIM
    heads = w // d
    hb = 6 if heads % 6 == 0 else (4 if heads % 4 == 0 else 1)
    tc = _tile(t, 128)
    has_s0 = s0 is not None
    tok = pl.BlockSpec((1, tc, hb * d), lambda b, hg, ti: (b, ti, hg))
    in_specs = [tok, tok, tok, tok, pl.BlockSpec((1, hb * d), lambda b, hg, ti: (0, hg))]
    args = [q, f, v, g, gain.reshape(1, w)]
    state_spec = pl.BlockSpec((1, hb, d, d), lambda b, hg, ti: (b, hg, 0, 0))
    if has_s0:
        in_specs.append(state_spec)
        args.append(s0)
    o, s_out = pl.pallas_call(
        functools.partial(_hgrn_body, n_heads=hb, t_chunk=tc, t_valid=t_valid, has_s0=has_s0),
        out_shape=[jax.ShapeDtypeStruct((bsz, t, out_width), BF16),
                   jax.ShapeDtypeStruct((bsz, heads, d, d), F32)],
        grid=(bsz, heads // hb, t // tc),
        in_specs=in_specs,
        out_specs=[tok, state_spec],
        scratch_shapes=[pltpu.VMEM((hb, d, d), F32), pltpu.VMEM((tc, hb * d), F32)],
        compiler_params=_cparams(("parallel", "parallel", "arbitrary")),
        name="hgrn2",
    )(*args)
    return o, s_out


def _memattn_body(q_ref, k_ref, v_ref, buf_ref, o_ref, *, n_heads, head_dim):
    del buf_ref
    scale = head_dim ** -0.5
    lanes = [slice(h * head_dim, (h + 1) * head_dim) for h in range(n_heads)]
    s = [lax.dot_general(q_ref[0, :, ls].astype(BF16), k_ref[0, :, ls].astype(BF16), NT_DIMS,
                         preferred_element_type=F32) * scale for ls in lanes]
    p = [jnp.exp(x - jnp.max(x, axis=-1, keepdims=True)) for x in s]
    p = [x / jnp.sum(x, axis=-1, keepdims=True) for x in p]
    o = [jnp.dot(x.astype(BF16), v_ref[0, :, ls].astype(BF16), preferred_element_type=F32)
         for x, ls in zip(p, lanes)]
    for x, ls in zip(o, lanes):
        o_ref[0, :, ls] = x.astype(o_ref.dtype)


def memory_attention(qm, mem_k, mem_v, layer, n_heads, o_buf, col_off):
    bsz, t, w = qm.shape
    m = mem_k.shape[2]
    tq = _tile(t, 256)
    assert col_off % w == 0
    cb = col_off // w
    kv_spec = pl.BlockSpec((None, 1, m, w), lambda b, i: (layer, b, 0, 0))
    return pl.pallas_call(
        functools.partial(_memattn_body, n_heads=n_heads, head_dim=w // n_heads),
        out_shape=jax.ShapeDtypeStruct(o_buf.shape, o_buf.dtype),
        grid=(bsz, t // tq),
        in_specs=[pl.BlockSpec((1, tq, w), lambda b, i: (b, i, 0)), kv_spec, kv_spec,
                  pl.BlockSpec(memory_space=pl.ANY)],
        out_specs=pl.BlockSpec((1, tq, w), lambda b, i: (b, i, cb)),
        input_output_aliases={3: 0},
        compiler_params=_cparams(("parallel", "parallel")),
        name="memory_attention",
    )(qm, mem_k, mem_v, o_buf)


def _cumsum_body(x_ref, c_ref, ct_ref, carry_ref):
    @pl.when(pl.program_id(1) == 0)
    def _():
        carry_ref[...] = jnp.zeros_like(carry_ref)

    n = x_ref.shape[1]
    tri = (lax.broadcasted_iota(jnp.int32, (n, n), 0) >= lax.broadcasted_iota(jnp.int32, (n, n), 1)).astype(F32)
    c = jnp.dot(tri, x_ref[0], precision=HIGHEST, preferred_element_type=F32) + carry_ref[...]
    carry_ref[...] = c[n - 1:n, :]
    c_ref[0] = c
    ct_ref[0] = c.T


def cumsum_tokens(x):
    bsz, t, w = x.shape
    tc = _tile(t, 256)
    return pl.pallas_call(
        _cumsum_body,
        out_shape=[jax.ShapeDtypeStruct((bsz, t, w), F32), jax.ShapeDtypeStruct((bsz, w, t), F32)],
        grid=(bsz, t // tc),
        in_specs=[pl.BlockSpec((1, tc, w), lambda b, i: (b, i, 0))],
        out_specs=[pl.BlockSpec((1, tc, w), lambda b, i: (b, i, 0)),
                   pl.BlockSpec((1, w, tc), lambda b, i: (b, 0, i))],
        scratch_shapes=[pltpu.VMEM((1, w), F32)],
        compiler_params=_cparams(("parallel", "arbitrary")),
        name="cumsum_tokens",
    )(x)


def _fox_prompt_body(q_ref, k_ref, v_ref, c_ref, ct_ref, o_ref, kb_ref, vb_ref, m_ref, l_ref, acc_ref,
                     *, tq, n_heads):
    hg = pl.program_id(1)
    i = pl.program_id(2)
    d = HEAD_DIM
    scale = d ** -0.5
    heads = range(n_heads)
    lanes = [slice(h * d, (h + 1) * d) for h in heads]

    @pl.when(i == 0)
    def _():
        kb_ref[...] = k_ref[0].astype(BF16)
        vb_ref[...] = v_ref[0].astype(BF16)

    q = q_ref[0].astype(BF16)
    lane = lax.broadcasted_iota(jnp.int32, (tq, LANES), 1)
    sub = lax.broadcasted_iota(jnp.int32, (SUBLANES, tq), 0)
    cq, ck_grp, ck_sel = [], [], []
    for h in heads:
        head = hg * n_heads + h
        cq.append(jnp.sum(jnp.where(lane == head, c_ref[0], 0.0), axis=1, keepdims=True))
        base = pl.multiple_of((head // SUBLANES) * SUBLANES, SUBLANES)
        ck_grp.append(pl.ds(base, SUBLANES))
        ck_sel.append(sub == (head - base))
    m_ref[...] = jnp.full_like(m_ref, -jnp.inf)
    l_ref[...] = jnp.zeros_like(l_ref)
    acc_ref[...] = jnp.zeros_like(acc_ref)

    def tile(j, masked):
        ks = pl.ds(pl.multiple_of(j * tq, tq), tq)
        s = [lax.dot_general(q[:, lanes[h]], kb_ref[ks, lanes[h]], NT_DIMS, preferred_element_type=F32) * scale
             for h in heads]
        s = [s[h] + cq[h] - _select_row(ct_ref[0, ck_grp[h], ks], ck_sel[h]) for h in heads]
        if masked:
            r = lax.broadcasted_iota(jnp.int32, (tq, tq), 0)
            cc = lax.broadcasted_iota(jnp.int32, (tq, tq), 1)
            s = [jnp.where(cc <= r, x, -jnp.inf) for x in s]
        m_old = [m_ref[:, lanes[h]] for h in heads]
        m_new = [jnp.maximum(m_old[h], jnp.max(s[h], axis=-1, keepdims=True)) for h in heads]
        p = [jnp.exp(s[h] - _lane_tiles(m_new[h], tq // LANES)) for h in heads]
        alpha = [jnp.exp(m_old[h] - m_new[h]) for h in heads]
        pv = [jnp.dot(p[h].astype(BF16), vb_ref[ks, lanes[h]], preferred_element_type=F32) for h in heads]
        for h in heads:
            l_ref[:, lanes[h]] = alpha[h] * l_ref[:, lanes[h]] + jnp.sum(p[h], axis=-1, keepdims=True)
            acc_ref[:, lanes[h]] = alpha[h] * acc_ref[:, lanes[h]] + pv[h]
            m_ref[:, lanes[h]] = m_new[h]

    def body(j, carry):
        tile(j, False)
        return carry

    lax.fori_loop(0, i, body, 0)
    tile(i, True)
    o_ref[0] = (acc_ref[...] / l_ref[...]).astype(o_ref.dtype)


def fox_attention_prompt(q, k, v, c, ct, out_width):
    bsz, t, w = q.shape
    d = HEAD_DIM
    heads = w // d
    hb = 4 if heads % 4 == 0 else 1
    tq = _tile(t, 256)
    blk = pl.BlockSpec((1, tq, hb * d), lambda b, h, i: (b, i, h))
    full = pl.BlockSpec((1, t, hb * d), lambda b, h, i: (b, 0, h))
    return pl.pallas_call(
        functools.partial(_fox_prompt_body, tq=tq, n_heads=hb),
        out_shape=jax.ShapeDtypeStruct((bsz, t, out_width), BF16),
        grid=(bsz, heads // hb, t // tq),
        in_specs=[blk, full, full,
                  pl.BlockSpec((1, tq, LANES), lambda b, h, i: (b, i, 0)),
                  pl.BlockSpec((1, LANES, t), lambda b, h, i: (b, 0, 0))],
        out_specs=blk,
        scratch_shapes=[pltpu.VMEM((t, hb * d), BF16), pltpu.VMEM((t, hb * d), BF16),
                        pltpu.VMEM((tq, hb * d), F32), pltpu.VMEM((tq, hb * d), F32),
                        pltpu.VMEM((tq, hb * d), F32)],
        compiler_params=_cparams(("parallel", "parallel", "arbitrary")),
        name="fox_attention_prompt",
    )(q, k, v, c, ct)


def _suffix_body(pt_ref, *refs, pages_per_step):
    del pt_ref
    x_refs = refs[:pages_per_step]
    d_ref, carry_ref = refs[pages_per_step:]

    @pl.when(pl.program_id(1) == 0)
    def _():
        carry_ref[...] = jnp.zeros_like(carry_ref)

    n = x_refs[0].shape[2]
    later = (lax.broadcasted_iota(jnp.int32, (n, n), 0) > lax.broadcasted_iota(jnp.int32, (n, n), 1)).astype(F32)
    within = [jnp.dot(x_ref[0], later, precision=HIGHEST, preferred_element_type=F32) for x_ref in x_refs]
    carry = carry_ref[...]
    for r in range(pages_per_step):
        d_ref[0, pages_per_step - 1 - r] = within[r] + carry
        carry = carry + jnp.sum(x_refs[r][0], axis=1, keepdims=True)
    carry_ref[...] = carry


def suffix_log_forget(logf_t, page_table):
    bsz, n_pages = page_table.shape
    _, heads, page = logf_t.shape
    r_pages = 16 if n_pages % 16 == 0 else 1
    steps = n_pages // r_pages

    def page_spec(r):
        return pl.BlockSpec((1, heads, page), lambda b, p, pt: (pt[b, n_pages - 1 - (p * r_pages + r)], 0, 0))

    grid_spec = pltpu.PrefetchScalarGridSpec(
        num_scalar_prefetch=1,
        grid=(bsz, steps),
        in_specs=[page_spec(r) for r in range(r_pages)],
        out_specs=pl.BlockSpec((1, r_pages, heads, page), lambda b, p, pt: (b, steps - 1 - p, 0, 0)),
        scratch_shapes=[pltpu.VMEM((heads, page), F32)],
    )
    return pl.pallas_call(
        functools.partial(_suffix_body, pages_per_step=r_pages),
        out_shape=jax.ShapeDtypeStruct((bsz, n_pages, heads, page), F32),
        grid_spec=grid_spec,
        compiler_params=_cparams(("parallel", "arbitrary")),
        name="suffix_log_forget",
    )(page_table, *([logf_t] * r_pages))


def _fox_decode_body(pt_ref, q_ref, *rest, n_heads, pages_per_step, page):
    del pt_ref
    r_pages = pages_per_step
    k_refs = rest[:r_pages]
    v_refs = rest[r_pages:2 * r_pages]
    (d_ref, cn_ref, cnt_ref, kn_ref, vn_ref, o_ref, m_ref, l_ref, acc_ref) = rest[2 * r_pages:]
    p_idx = pl.program_id(1)
    d = HEAD_DIM
    tq = q_ref.shape[1]
    scale = d ** -0.5
    heads = range(n_heads)
    lanes = [slice(h * d, (h + 1) * d) for h in heads]

    @pl.when(p_idx == 0)
    def _():
        m_ref[...] = jnp.full_like(m_ref, -jnp.inf)
        l_ref[...] = jnp.zeros_like(l_ref)
        acc_ref[...] = jnp.zeros_like(acc_ref)

    def update(s, v_tiles):
        m_old = [m_ref[:, lanes[h]] for h in heads]
        m_new = [jnp.maximum(m_old[h], jnp.max(s[h], axis=-1, keepdims=True)) for h in heads]
        p = [jnp.exp(s[h] - _lane_tiles(m_new[h], s[h].shape[1] // LANES)) for h in heads]
        alpha = [jnp.exp(m_old[h] - m_new[h]) for h in heads]
        pb = [x.astype(BF16) for x in p]
        pv = []
        for h in heads:
            acc = jnp.dot(pb[h][:, 0:d], v_tiles[h][0], preferred_element_type=F32)
            for r in range(1, len(v_tiles[h])):
                acc = acc + jnp.dot(pb[h][:, r * d:(r + 1) * d], v_tiles[h][r], preferred_element_type=F32)
            pv.append(acc)
        for h in heads:
            l_ref[:, lanes[h]] = alpha[h] * l_ref[:, lanes[h]] + jnp.sum(p[h], axis=-1, keepdims=True)
            acc_ref[:, lanes[h]] = alpha[h] * acc_ref[:, lanes[h]] + pv[h]
            m_ref[:, lanes[h]] = m_new[h]

    qb = q_ref[0].astype(BF16)

    def head_rows(ref, h):
        return ref[0, pl.ds(h, page, stride=n_heads), :].astype(BF16)

    s = []
    for h in heads:
        parts = [lax.dot_general(qb[:, lanes[h]], head_rows(k_refs[r], h), NT_DIMS, preferred_element_type=F32) * scale
                 + (cn_ref[0, :, h:h + 1] + d_ref[0, r, h:h + 1, :]) for r in range(r_pages)]
        s.append(parts[0] if r_pages == 1 else jnp.concatenate(parts, axis=1))
    update(s, [[head_rows(v_refs[r], h) for r in range(r_pages)] for h in heads])

    @pl.when(p_idx == pl.num_programs(1) - 1)
    def _():
        n_new = kn_ref.shape[1]
        r_i = lax.broadcasted_iota(jnp.int32, (tq, n_new), 0)
        c_i = lax.broadcasted_iota(jnp.int32, (tq, n_new), 1)
        s_new = [lax.dot_general(qb[:, lanes[h]], kn_ref[0, :, lanes[h]].astype(BF16), NT_DIMS,
                                 preferred_element_type=F32) * scale
                 + (cn_ref[0, :, h:h + 1] - cnt_ref[0, h:h + 1, :]) for h in heads]
        s_new = [jnp.where(c_i <= r_i, x, -jnp.inf) for x in s_new]
        update(s_new, [[vn_ref[0, :, lanes[h]].astype(BF16)] for h in heads])
        o_ref[0] = (acc_ref[...] / l_ref[...]).astype(o_ref.dtype)


def fox_attention_decode(q, cache_k, cache_v, d_suffix, cn, cnt, k_new, v_new, page_table, out_width):
    bsz, tq, w = q.shape
    heads = w // HEAD_DIM
    rows = cache_k.shape[1]
    page = rows // heads
    n_pages = page_table.shape[1]
    tn = k_new.shape[1]
    r_pages = 4 if n_pages % 4 == 0 else 1
    steps = n_pages // r_pages

    def page_spec(r):
        return pl.BlockSpec((1, rows, HEAD_DIM), lambda b, p, pt: (pt[b, p * r_pages + r], 0, 0))

    whole = lambda shape: pl.BlockSpec((1,) + shape, lambda b, p, pt: (b, 0, 0))
    grid_spec = pltpu.PrefetchScalarGridSpec(
        num_scalar_prefetch=1,
        grid=(bsz, steps),
        in_specs=[whole((tq, w))] + [page_spec(r) for r in range(r_pages)] * 2
                 + [pl.BlockSpec((1, r_pages, heads, page), lambda b, p, pt: (b, p, 0, 0)),
                    pl.BlockSpec((1, tq, LANES), lambda b, p, pt: (b, 0, 0)),
                    whole((LANES, tn)), whole((tn, w)), whole((tn, w))],
        out_specs=pl.BlockSpec((1, tq, w), lambda b, p, pt: (b, 0, 0)),
        scratch_shapes=[pltpu.VMEM((tq, w), F32), pltpu.VMEM((tq, w), F32), pltpu.VMEM((tq, w), F32)],
    )
    return pl.pallas_call(
        functools.partial(_fox_decode_body, n_heads=heads, pages_per_step=r_pages, page=page),
        out_shape=jax.ShapeDtypeStruct((bsz, tq, out_width), BF16),
        grid_spec=grid_spec,
        compiler_params=_cparams(("parallel", "arbitrary")),
        name="fox_attention_decode",
    )(page_table, q, *([cache_k] * r_pages), *([cache_v] * r_pages), d_suffix, cn, cnt, k_new, v_new)


def _pad_rows(x, rows):
    pad = [(0, 0)] * x.ndim
    pad[1] = (0, rows - x.shape[1])
    return jnp.pad(x, pad)


def _trunk(hp, hs, dims_p, dims_s, mem_p, mem_s, hg_s0, decode, wts):
    d_model = hp.shape[1]
    tok_w = wts["tok_w"]
    mem_w = d_model - tok_w
    mem_heads = mem_p[0].shape[3]
    to4 = lambda a: a.reshape(a.shape[:3] + (mem_w,))
    mem_p = (to4(mem_p[0]), to4(mem_p[1]))
    mem_s = (to4(mem_s[0]), to4(mem_s[1]))
    n_a = wts["w_in_a"].shape[0]
    depth = wts["w_out"].shape[0]
    (bp, tp), (bs, ts) = dims_p, dims_s
    states_p, states_s = [], []
    hn_p = rmsnorm_bf16(hp, wts["norm_pre_mix"][0])
    hn_s = rmsnorm_bf16(hs, wts["norm_pre_mix"][0])
    kv_p = kv_s = None
    hkv_p = hkv_s = None
    for layer in range(depth):
        if layer == n_a:
            wkv = wts["w_kv"]
            k_p, k_s = matmul(hkv_p, hkv_s, wkv, 0, tok_w)
            v_p, v_s = matmul(hkv_p, hkv_s, wkv, tok_w, tok_w)
            lf_p, lf_s = matmul(hkv_p, hkv_s, wts["w_fg"], 0, LANES, kind="logsig", vecs=(wts["b_fg"],))
            kv_p, kv_s = (k_p, v_p, lf_p), (k_s, v_s, lf_s)
        if layer < n_a:
            w_in = wts["w_in_a"]
            q = matmul(hn_p, hn_s, w_in, 0, tok_w, kind="silu", layer=layer)
            f = matmul(hn_p, hn_s, w_in, tok_w, tok_w, kind="fgate", vecs=(wts["lb"][layer],), layer=layer)
            v = matmul(hn_p, hn_s, w_in, 2 * tok_w, tok_w, layer=layer)
            g = matmul(hn_p, hn_s, w_in, 3 * tok_w, tok_w, kind="silu", layer=layer)
            zm_p, zm_s = matmul(hn_p, hn_s, w_in, 4 * tok_w, mem_w, layer=layer)
            gain = wts["hg_out_norm"][layer]
            shp_p, shp_s = (bp, tp, tok_w), (bs, ts, tok_w)
            o_p, st_p = hgrn2(q[0].reshape(shp_p), f[0].reshape(shp_p), v[0].reshape(shp_p), g[0].reshape(shp_p),
                              gain, None, d_model)
            o_s, st_s = hgrn2(q[1].reshape(shp_s), f[1].reshape(shp_s), v[1].reshape(shp_s), g[1].reshape(shp_s),
                              gain, hg_s0[layer], d_model, wts["t_valid"])
            states_p.append(st_p)
            states_s.append(st_s)
        else:
            w_in = wts["w_in_b"]
            q_p, q_s = matmul(hn_p, hn_s, w_in, 0, tok_w, layer=layer - n_a)
            zm_p, zm_s = matmul(hn_p, hn_s, w_in, tok_w, mem_w, layer=layer - n_a)
            c, ct = cumsum_tokens(kv_p[2].reshape(bp, tp, LANES))
            o_p = fox_attention_prompt(q_p.reshape(bp, tp, tok_w), kv_p[0].reshape(bp, tp, tok_w),
                                       kv_p[1].reshape(bp, tp, tok_w), c, ct, d_model)
            cn, cnt = cumsum_tokens(_pad_rows(kv_s[2].reshape(bs, ts, LANES), LANES))
            kn = _pad_rows(kv_s[0].reshape(bs, ts, tok_w), LANES)
            vn = _pad_rows(kv_s[1].reshape(bs, ts, tok_w), LANES)
            o_s = fox_attention_decode(q_s.reshape(bs, ts, tok_w), decode["cache_k"], decode["cache_v"],
                                       decode["d_suffix"], cn, cnt, kn, vn, decode["page_table"], d_model)
        o_p = memory_attention(zm_p.reshape(bp, tp, mem_w), mem_p[0], mem_p[1], layer, mem_heads, o_p, tok_w)
        o_s = memory_attention(zm_s.reshape(bs, ts, mem_w), mem_s[0], mem_s[1], layer, mem_heads, o_s, tok_w)
        mixed_p, mixed_s = matmul(o_p.reshape(bp * tp, d_model), o_s.reshape(bs * ts, d_model),
                                  wts["w_out"], 0, d_model, layer=layer)
        hp, (hn_p,) = post_norm_residual(mixed_p, hp, wts["norm_post_mix"][layer], [wts["norm_pre_mlp"][layer]])
        hs, (hn_s,) = post_norm_residual(mixed_s, hs, wts["norm_post_mix"][layer], [wts["norm_pre_mlp"][layer]])
        u_p, u_s = matmul(hn_p, hn_s, wts["w_mlp_up"], 0, wts["w_mlp_up"].shape[2],
                          kind="relu2", out_dtype=BF16, layer=layer)
        down_p, down_s = matmul_long_k(u_p, u_s, wts["w_mlp_down"], layer)
        nxt = []
        if layer + 1 < depth:
            nxt = [wts["norm_pre_mix"][layer + 1]] + ([wts["kv_norm"]] if layer + 1 == n_a else [])
        hp, hns_p = post_norm_residual(down_p, hp, wts["norm_post_mlp"][layer], nxt)
        hs, hns_s = post_norm_residual(down_s, hs, wts["norm_post_mlp"][layer], nxt)
        if nxt:
            hn_p, hn_s = hns_p[0], hns_s[0]
            if layer + 1 == n_a:
                hkv_p, hkv_s = hns_p[1], hns_s[1]
    return (hp, jnp.stack(states_p), kv_p), (hs, jnp.stack(states_s), kv_s)


def kernel(x_prompt, x_sample, state_hgrn, cache_k, cache_v, cache_logf, cache_mem_k, cache_mem_v, page_table, mem_prompt, norm_pre_mix, norm_post_mix, norm_pre_mlp, norm_post_mlp, w_in_a, hg_lb_logits, hg_out_norm, w_in_b, kv_norm, w_kv_shared, b_fgate, w_out, mem_norm, w_mem_kv, w_mlp_up, w_mlp_down):
    bp, tp, d_model = x_prompt.shape
    bs, ts, _ = x_sample.shape
    fox_heads, fox_dim = cache_k.shape[2], cache_k.shape[3]
    assert fox_dim == HEAD_DIM and state_hgrn.shape[-1] == HEAD_DIM
    tok_w = fox_heads * fox_dim
    mem_w = d_model - tok_w
    mem_len = mem_prompt.shape[1]
    depth = w_out.shape[0]
    n_pool, page = cache_k.shape[0], cache_k.shape[1]
    t_pad = max(SUBLANES, ts)

    lb_all = jnp.cumsum(jax.nn.softmax(hg_lb_logits.astype(F32), axis=0), axis=0)
    wts = {
        "tok_w": tok_w, "t_valid": ts,
        "norm_pre_mix": norm_pre_mix, "norm_post_mix": norm_post_mix,
        "norm_pre_mlp": norm_pre_mlp, "norm_post_mlp": norm_post_mlp,
        "kv_norm": kv_norm, "hg_out_norm": hg_out_norm, "lb": lb_all,
        "w_in_a": w_in_a, "w_in_b": w_in_b,
        "w_kv": w_kv_shared[None],
        "w_fg": jnp.pad(w_kv_shared[:, 2 * tok_w:], ((0, 0), (0, LANES - fox_heads)))[None],
        "b_fg": jnp.pad(b_fgate.astype(F32), (0, LANES - fox_heads)),
        "w_out": w_out, "w_mlp_up": w_mlp_up, "w_mlp_down": w_mlp_down.astype(BF16),
    }

    mem_n = rmsnorm_bf16(mem_prompt.reshape(bp * mem_len, d_model), mem_norm)
    mem_k_flat = matmul(mem_n, None, w_mem_kv, 0, mem_w, all_layers=True)
    mem_v_flat = matmul(mem_n, None, w_mem_kv, mem_w, mem_w, all_layers=True)
    mem_heads = cache_mem_k.shape[3]
    mem_shape = (depth, bp, mem_len, mem_heads, mem_w // mem_heads)
    mem_k_prompt = mem_k_flat.reshape(mem_shape)
    mem_v_prompt = mem_v_flat.reshape(mem_shape)

    logf_t = cache_logf.transpose(0, 2, 1)
    decode = {
        "cache_k": cache_k.reshape(n_pool, page * fox_heads, fox_dim),
        "cache_v": cache_v.reshape(n_pool, page * fox_heads, fox_dim),
        "d_suffix": suffix_log_forget(logf_t, page_table),
        "page_table": page_table,
    }
    xs = _pad_rows(x_sample, t_pad).reshape(bs * t_pad, d_model)
    (y_p, hg_p, (k_p, v_p, lf_p)), (y_s, hg_s, (k_s, v_s, lf_s)) = _trunk(
        x_prompt.reshape(bp * tp, d_model), xs, (bp, tp), (bs, t_pad),
        (mem_k_prompt, mem_v_prompt), (cache_mem_k, cache_mem_v), state_hgrn, decode, wts)

    def unpad(a, width):
        return a.reshape(bs, t_pad, width)[:, :ts]

    return (y_p.reshape(bp, tp, d_model),
            unpad(y_s, d_model),
            hg_p.astype(state_hgrn.dtype),
            hg_s.astype(state_hgrn.dtype),
            k_p.reshape(bp, tp, fox_heads, fox_dim),
            v_p.reshape(bp, tp, fox_heads, fox_dim),
            lf_p[:, :fox_heads].reshape(bp, tp, fox_heads),
            unpad(k_s, tok_w).reshape(bs, ts, fox_heads, fox_dim),
            unpad(v_s, tok_w).reshape(bs, ts, fox_heads, fox_dim),
            unpad(lf_s, LANES)[:, :, :fox_heads],
            mem_k_prompt, mem_v_prompt)
```

```python
import functools

import jax
import jax.numpy as jnp
from jax import lax
from jax.experimental import pallas as pl
from jax.experimental.pallas import tpu as pltpu

F32 = jnp.float32
BF16 = jnp.bfloat16

NORM_EPS = 1e-6
HEAD_DIM = 128
LANES = 128
SUBLANES = 8
VMEM_LIMIT = 52 * 1024 * 1024
HG_CHUNK = 64
HG_SAFE_DECAY = 60.0
HIGHEST = lax.Precision.HIGHEST
LOG2_E = 1.4426950408889634

NT_DIMS = (((1,), (1,)), ((), ()))
TN_DIMS = (((0,), (0,)), ((), ()))


def _cparams(semantics):
    return pltpu.CompilerParams(dimension_semantics=semantics, vmem_limit_bytes=VMEM_LIMIT)


def _tile(n, want):
    if n <= want:
        return n
    t = want
    while t >= 16:
        if n % t == 0:
            return t
        t -= 16
    return n


def _lane_tiles(x, n):
    return x if n == 1 else jnp.concatenate([x] * n, axis=1)


def _select_row(group, sel):
    return jnp.sum(jnp.where(sel, group, 0.0), axis=0, keepdims=True)


def _rms_scale(x):
    return lax.rsqrt(jnp.mean(x * x, axis=-1, keepdims=True) + NORM_EPS)


def _rmsnorm_body(x_ref, g_ref, o_ref):
    x = x_ref[...]
    o_ref[...] = (x * _rms_scale(x) * g_ref[...]).astype(o_ref.dtype)


def rmsnorm_bf16(x, gain):
    m, d = x.shape
    tm = _tile(m, 256)
    return pl.pallas_call(
        _rmsnorm_body,
        out_shape=jax.ShapeDtypeStruct((m, d), BF16),
        grid=(m // tm,),
        in_specs=[pl.BlockSpec((tm, d), lambda i: (i, 0)), pl.BlockSpec((1, d), lambda i: (0, 0))],
        out_specs=pl.BlockSpec((tm, d), lambda i: (i, 0)),
        compiler_params=_cparams(("parallel",)),
        name="rmsnorm",
    )(x, gain.reshape(1, d))


def _post_body(m_ref, h_ref, gp_ref, *rest, n_next):
    g_next = rest[:n_next]
    ho_ref = rest[n_next]
    hn_refs = rest[n_next + 1:]
    m = m_ref[...]
    h = h_ref[...] + m * _rms_scale(m) * gp_ref[...]
    ho_ref[...] = h
    if n_next:
        hs = h * _rms_scale(h)
        for g_ref, hn_ref in zip(g_next, hn_refs):
            hn_ref[...] = (hs * g_ref[...]).astype(hn_ref.dtype)


def post_norm_residual(mixed, h, gain_post, next_gains):
    m, d = h.shape
    tm = _tile(m, 256)
    n_next = len(next_gains)
    row = pl.BlockSpec((tm, d), lambda i: (i, 0))
    vec = pl.BlockSpec((1, d), lambda i: (0, 0))
    outs = pl.pallas_call(
        functools.partial(_post_body, n_next=n_next),
        out_shape=[jax.ShapeDtypeStruct((m, d), F32)] + [jax.ShapeDtypeStruct((m, d), BF16)] * n_next,
        grid=(m // tm,),
        in_specs=[row, row, vec] + [vec] * n_next,
        out_specs=[row] * (1 + n_next),
        compiler_params=_cparams(("parallel",)),
        name="post_norm_residual",
    )(mixed, h, gain_post.reshape(1, d), *[g.reshape(1, d) for g in next_gains])
    return outs[0], list(outs[1:])


def _epilogue(kind, acc, vecs):
    if kind == "none":
        return acc
    if kind == "silu":
        return acc * jax.nn.sigmoid(acc)
    if kind == "fgate":
        lb = vecs[0][...]
        return lb + (1.0 - lb) * jax.nn.sigmoid(acc)
    if kind == "relu2":
        r = jnp.maximum(acc, 0.0)
        return r * r
    if kind == "logsig":
        return jax.nn.log_sigmoid(acc + vecs[0][...])
    raise ValueError(kind)


def _mm_body(*refs, kind, n_vec, has_second, cast_w):
    if has_second:
        x_ref, x2_ref, w_ref = refs[:3]
        rest = refs[3:]
    else:
        x_ref, w_ref = refs[:2]
        x2_ref = None
        rest = refs[2:]
    vecs = rest[:n_vec]
    outs = rest[n_vec:]
    o_ref = outs[0]
    i = pl.program_id(2)
    if cast_w:
        wb_ref = outs[-1]

        @pl.when(i == 0)
        def _():
            wb_ref[...] = w_ref[...].astype(BF16)
    else:
        wb_ref = w_ref

    acc = jnp.dot(x_ref[...], wb_ref[...], preferred_element_type=F32)
    o_ref[...] = _epilogue(kind, acc, vecs).astype(o_ref.dtype)

    if has_second:
        o2_ref = outs[1]

        @pl.when(i == pl.num_programs(2) - 1)
        def _():
            acc2 = jnp.dot(x2_ref[...], wb_ref[...], preferred_element_type=F32)
            o2_ref[...] = _epilogue(kind, acc2, vecs).astype(o2_ref.dtype)


def matmul(x, x2, w, col_off, n_cols, kind="none", vecs=(), out_dtype=F32, layer=0, all_layers=False):
    m, k = x.shape
    nl = w.shape[0] if all_layers else 1
    cast_w = w.dtype != BF16
    tm = _tile(m, 1024)
    tn = _tile(n_cols, 512 if cast_w else 1024)
    assert col_off % tn == 0 and n_cols % tn == 0
    jo = col_off // tn
    has_second = x2 is not None
    vec_arrs = [v.reshape(1, n_cols).astype(F32) for v in vecs]
    in_specs = [pl.BlockSpec((tm, k), lambda l, j, i: (i, 0))]
    args = [x]
    out_shape = [jax.ShapeDtypeStruct((nl, m, n_cols), out_dtype)]
    out_specs = [pl.BlockSpec((None, tm, tn), lambda l, j, i: (l, i, j))]
    if has_second:
        m2 = x2.shape[0]
        in_specs.append(pl.BlockSpec((m2, k), lambda l, j, i: (0, 0)))
        args.append(x2)
        out_shape.append(jax.ShapeDtypeStruct((nl, m2, n_cols), out_dtype))
        out_specs.append(pl.BlockSpec((None, m2, tn), lambda l, j, i: (l, 0, j)))
    in_specs.append(pl.BlockSpec((None, k, tn), lambda l, j, i: (l if all_layers else layer, 0, jo + j)))
    in_specs += [pl.BlockSpec((1, tn), lambda l, j, i: (0, j))] * len(vec_arrs)
    outs = pl.pallas_call(
        functools.partial(_mm_body, kind=kind, n_vec=len(vec_arrs), has_second=has_second, cast_w=cast_w),
        out_shape=out_shape,
        grid=(nl, n_cols // tn, m // tm),
        in_specs=in_specs,
        out_specs=out_specs,
        scratch_shapes=[pltpu.VMEM((k, tn), BF16)] if cast_w else [],
        compiler_params=_cparams(("parallel", "parallel", "arbitrary")),
        name="matmul_" + kind,
    )(*args, w, *vec_arrs)
    if all_layers:
        return outs[0]
    return (outs[0][0], outs[1][0]) if has_second else outs[0][0]


def _mm_acc_body(x_ref, x2_ref, w_ref, o_ref, o2_ref):
    i = pl.program_id(1)
    kk = pl.program_id(2)

    @pl.when(kk == 0)
    def _():
        o_ref[...] = jnp.zeros_like(o_ref)

    o_ref[...] += jnp.dot(x_ref[...], w_ref[...], preferred_element_type=F32)

    first = i == 0

    @pl.when(jnp.logical_and(first, kk == 0))
    def _():
        o2_ref[...] = jnp.zeros_like(o2_ref)

    @pl.when(first)
    def _():
        o2_ref[...] += jnp.dot(x2_ref[...], w_ref[...], preferred_element_type=F32)


def matmul_long_k(x, x2, w, layer):
    m, k = x.shape
    m2 = x2.shape[0]
    n = w.shape[2]
    tm = _tile(m, 1024)
    tn = _tile(n, 1024)
    tk = _tile(k, 4096)
    return pl.pallas_call(
        _mm_acc_body,
        out_shape=[jax.ShapeDtypeStruct((m, n), F32), jax.ShapeDtypeStruct((m2, n), F32)],
        grid=(n // tn, m // tm, k // tk),
        in_specs=[pl.BlockSpec((tm, tk), lambda j, i, kk: (i, kk)),
                  pl.BlockSpec((m2, tk), lambda j, i, kk: (0, kk)),
                  pl.BlockSpec((None, tk, tn), lambda j, i, kk: (layer, kk, j))],
        out_specs=[pl.BlockSpec((tm, tn), lambda j, i, kk: (i, j)),
                   pl.BlockSpec((m2, tn), lambda j, i, kk: (0, j))],
        compiler_params=_cparams(("parallel", "arbitrary", "arbitrary")),
        name="matmul_long_k",
    )(x, x2, w)


def _side_cast_plan(jobs, n_steps, step_of):
    in_specs, out_specs, out_shapes, arrays = [], [], [], []
    for w, layer in jobs:
        _, k, c = w.shape
        rows = next(r for r in range(16, k + 1, 16) if k % r == 0 and k // r <= n_steps)
        last = k // rows - 1
        in_specs.append(pl.BlockSpec(
            (None, rows, c), lambda *g, layer=layer, last=last: (layer, jnp.minimum(step_of(*g), last), 0)))
        out_specs.append(pl.BlockSpec((rows, c), lambda *g, last=last: (jnp.minimum(step_of(*g), last), 0)))
        out_shapes.append(jax.ShapeDtypeStruct((k, c), BF16))
        arrays.append(w)
    return in_specs, out_specs, out_shapes, arrays


def _side_cast(src_refs, dst_refs):
    for src, dst in zip(src_refs, dst_refs):
        dst[...] = src[...].astype(BF16)


def _hgrn_body(*refs, n_heads, t_chunk, t_valid, has_s0, n_side):
    q_ref, f_ref, v_ref, g_ref, gain_ref = refs[:5]
    refs = refs[5:]
    s0_ref = None
    if has_s0:
        s0_ref, refs = refs[0], refs[1:]
    side_in, refs = refs[:n_side], refs[n_side:]
    o_ref, sout_ref = refs[:2]
    side_out = refs[2:2 + n_side]
    st_ref, o_scr = refs[2 + n_side:]
    _side_cast(side_in, side_out)
    t_idx = pl.program_id(2)
    d = HEAD_DIM
    c = min(HG_CHUNK, t_chunk)
    n_sub = t_chunk // c
    heads = range(n_heads)
    lanes = [slice(h * d, (h + 1) * d) for h in heads]

    @pl.when(t_idx == 0)
    def _():
        for h in heads:
            if has_s0:
                st_ref[h] = s0_ref[0, h].T
            else:
                st_ref[h] = jnp.zeros((d, d), F32)

    if t_valid is not None:
        assert n_sub == 1 and t_valid <= t_chunk
        o_scr[...] = jnp.zeros_like(o_scr)
        sts = [st_ref[h] for h in heads]
        for t in range(t_valid):
            f_t = f_ref[0, t:t + 1, :]
            v_t = v_ref[0, t:t + 1, :]
            q_t = q_ref[0, t:t + 1, :]
            v_cols = [jnp.broadcast_to(v_t[:, lanes[h]], (d, d)).T for h in heads]
            sts = [sts[h] * f_t[:, lanes[h]] + v_cols[h] * (1.0 - f_t[:, lanes[h]]) for h in heads]
            q8s = [jnp.broadcast_to(q_t[:, lanes[h]], (SUBLANES, d)).astype(BF16) for h in heads]
            o8s = [lax.dot_general(q8s[h], sts[h].astype(BF16), NT_DIMS, preferred_element_type=F32) for h in heads]
            for h in heads:
                o_scr[t:t + 1, lanes[h]] = o8s[h][0:1, :]
        for h in heads:
            st_ref[h] = sts[h]

    def token_steps(h, first, count):
        hs = pl.ds(h * d, d)

        def step(t, st):
            base = pl.multiple_of(((first + t) // SUBLANES) * SUBLANES, SUBLANES)
            grp = pl.ds(base, SUBLANES)
            sel = lax.broadcasted_iota(jnp.int32, (SUBLANES, d), 0) == (first + t - base)
            f_t = _select_row(f_ref[0, grp, hs], sel)
            v_col = jnp.broadcast_to(_select_row(v_ref[0, grp, hs], sel), (d, d)).T
            st = st * f_t + v_col * (1.0 - f_t)
            q8 = jnp.broadcast_to(_select_row(q_ref[0, grp, hs], sel), (SUBLANES, d)).astype(BF16)
            o8 = lax.dot_general(q8, st.astype(BF16), NT_DIMS, preferred_element_type=F32)
            o_scr[grp, hs] = jnp.where(sel, o8, o_scr[grp, hs])
            return st

        st_ref[h] = lax.fori_loop(0, count, step, st_ref[h])

    row = lax.broadcasted_iota(jnp.int32, (c, c), 0)
    col = lax.broadcasted_iota(jnp.int32, (c, c), 1)
    causal = row >= col
    tri = causal.astype(F32)

    for sc in range(n_sub if t_valid is None else 0):
        rows = pl.ds(sc * c, c)
        f_all = f_ref[0, rows, :]
        lg_all = jnp.log(f_all)
        total_decay = jnp.min(jnp.sum(lg_all, axis=0, keepdims=True))
        safe = total_decay > -HG_SAFE_DECAY

        @pl.when(safe)
        def _():
            b = jnp.dot(tri, jnp.log(f_ref[0, rows, :]), precision=HIGHEST, preferred_element_type=F32)
            e_last = jnp.exp(b[c - 1:c, :])
            qt = (q_ref[0, rows, :] * jnp.exp(b)).astype(BF16)
            kt = (1.0 - f_ref[0, rows, :]) * jnp.exp(-b)
            kd = (kt * e_last).astype(BF16)
            kt = kt.astype(BF16)
            vb = v_ref[0, rows, :].astype(BF16)
            stb = [st_ref[h].astype(BF16) for h in heads]
            att = [lax.dot_general(qt[:, lanes[h]], kt[:, lanes[h]], NT_DIMS, preferred_element_type=F32)
                   for h in heads]
            o_inter = [lax.dot_general(qt[:, lanes[h]], stb[h], NT_DIMS, preferred_element_type=F32) for h in heads]
            upd = [lax.dot_general(vb[:, lanes[h]], kd[:, lanes[h]], TN_DIMS, preferred_element_type=F32)
                   for h in heads]
            att = [jnp.where(causal, a, 0.0).astype(BF16) for a in att]
            o_intra = [jnp.dot(att[h], vb[:, lanes[h]], preferred_element_type=F32) for h in heads]
            for h in heads:
                o_scr[rows, lanes[h]] = o_intra[h] + o_inter[h]
                st_ref[h] = st_ref[h] * e_last[:, lanes[h]] + upd[h]

        @pl.when(jnp.logical_not(safe))
        def _():
            for h in heads:
                token_steps(h, sc * c, c)

    for h in heads:
        o = o_scr[:, lanes[h]]
        o = o * _rms_scale(o) * gain_ref[:, lanes[h]]
        o_ref[0, :, lanes[h]] = (o * g_ref[0, :, lanes[h]]).astype(o_ref.dtype)

    @pl.when(t_idx == pl.num_programs(2) - 1)
    def _():
        for h in heads:
            sout_ref[0, h] = st_ref[h].T


def hgrn2(q, f, v, g, gain, s0, out_width, t_valid=None, cast_jobs=()):
    bsz, t, w = q.shape
    d = HEAD_DIM
    heads = w // d
    hb = 6 if heads % 6 == 0 else (4 if heads % 4 == 0 else 1)
    tc = _tile(t, 128)
    has_s0 = s0 is not None
    tok = pl.BlockSpec((1, tc, hb * d), lambda b, hg, ti: (b, ti, hg))
    in_specs = [tok, tok, tok, tok, pl.BlockSpec((1, hb * d), lambda b, hg, ti: (0, hg))]
    args = [q, f, v, g, gain.reshape(1, w)]
    state_spec = pl.BlockSpec((1, hb, d, d), lambda b, hg, ti: (b, hg, 0, 0))
    if has_s0:
        in_specs.append(state_spec)
        args.append(s0)
    grid = (bsz, heads // hb, t // tc)
    side = _side_cast_plan(cast_jobs, grid[0] * grid[1] * grid[2],
                           lambda b, hg, ti: (b * grid[1] + hg) * grid[2] + ti)
    outs = pl.pallas_call(
        functools.partial(_hgrn_body, n_heads=hb, t_chunk=tc, t_valid=t_valid, has_s0=has_s0,
                          n_side=len(cast_jobs)),
        out_shape=[jax.ShapeDtypeStruct((bsz, t, out_width), BF16),
                   jax.ShapeDtypeStruct((bsz, heads, d, d), F32)] + side[2],
        grid=grid,
        in_specs=in_specs + side[0],
        out_specs=[tok, state_spec] + side[1],
        scratch_shapes=[pltpu.VMEM((hb, d, d), F32), pltpu.VMEM((tc, hb * d), F32)],
        compiler_params=_cparams(("arbitrary", "arbitrary", "arbitrary")),
        name="hgrn2",
    )(*args, *side[3])
    return outs[0], outs[1], list(outs[2:])


def _memattn_body(q_ref, k_ref, v_ref, buf_ref, o_ref, *, n_heads, head_dim):
    del buf_ref
    scale = head_dim ** -0.5
    lanes = [slice(h * head_dim, (h + 1) * head_dim) for h in range(n_heads)]
    s = [lax.dot_general(q_ref[0, :, ls].astype(BF16), k_ref[0, :, ls].astype(BF16), NT_DIMS,
                         preferred_element_type=F32) * scale for ls in lanes]
    p = [jnp.exp(x - jnp.max(x, axis=-1, keepdims=True)) for x in s]
    p = [x / jnp.sum(x, axis=-1, keepdims=True) for x in p]
    o = [jnp.dot(x.astype(BF16), v_ref[0, :, ls].astype(BF16), preferred_element_type=F32)
         for x, ls in zip(p, lanes)]
    for x, ls in zip(o, lanes):
        o_ref[0, :, ls] = x.astype(o_ref.dtype)


def memory_attention(qm, mem_k, mem_v, layer, n_heads, o_buf, col_off):
    bsz, t, w = qm.shape
    m = mem_k.shape[2]
    tq = _tile(t, 256)
    assert col_off % w == 0
    cb = col_off // w
    kv_spec = pl.BlockSpec((None, 1, m, w), lambda b, i: (layer, b, 0, 0))
    return pl.pallas_call(
        functools.partial(_memattn_body, n_heads=n_heads, head_dim=w // n_heads),
        out_shape=jax.ShapeDtypeStruct(o_buf.shape, o_buf.dtype),
        grid=(bsz, t // tq),
        in_specs=[pl.BlockSpec((1, tq, w), lambda b, i: (b, i, 0)), kv_spec, kv_spec,
                  pl.BlockSpec(memory_space=pl.ANY)],
        out_specs=pl.BlockSpec((1, tq, w), lambda b, i: (b, i, cb)),
        input_output_aliases={3: 0},
        compiler_params=_cparams(("parallel", "parallel")),
        name="memory_attention",
    )(qm, mem_k, mem_v, o_buf)


def _cumsum_body(x_ref, c_ref, ct_ref, carry_ref):
    @pl.when(pl.program_id(1) == 0)
    def _():
        carry_ref[...] = jnp.zeros_like(carry_ref)

    n = x_ref.shape[1]
    tri = (lax.broadcasted_iota(jnp.int32, (n, n), 0) >= lax.broadcasted_iota(jnp.int32, (n, n), 1)).astype(F32)
    c = jnp.dot(tri, x_ref[0], precision=HIGHEST, preferred_element_type=F32) + carry_ref[...]
    carry_ref[...] = c[n - 1:n, :]
    c_ref[0] = c
    ct_ref[0] = c.T


def cumsum_tokens(x):
    bsz, t, w = x.shape
    tc = _tile(t, 256)
    return pl.pallas_call(
        _cumsum_body,
        out_shape=[jax.ShapeDtypeStruct((bsz, t, w), F32), jax.ShapeDtypeStruct((bsz, w, t), F32)],
        grid=(bsz, t // tc),
        in_specs=[pl.BlockSpec((1, tc, w), lambda b, i: (b, i, 0))],
        out_specs=[pl.BlockSpec((1, tc, w), lambda b, i: (b, i, 0)),
                   pl.BlockSpec((1, w, tc), lambda b, i: (b, 0, i))],
        scratch_shapes=[pltpu.VMEM((1, w), F32)],
        compiler_params=_cparams(("parallel", "arbitrary")),
        name="cumsum_tokens",
    )(x)


def _fox_prompt_body(q_ref, k_ref, v_ref, c_ref, ct_ref, *refs, tq, n_heads, n_side):
    side_in, refs = refs[:n_side], refs[n_side:]
    o_ref = refs[0]
    side_out = refs[1:1 + n_side]
    kb_ref, vb_ref, m_ref, l_ref, acc_ref = refs[1 + n_side:]
    _side_cast(side_in, side_out)
    hg = pl.program_id(1)
    i = pl.program_id(2)
    d = HEAD_DIM
    scale2 = d ** -0.5 * LOG2_E
    heads = range(n_heads)
    lanes = [slice(h * d, (h + 1) * d) for h in heads]

    @pl.when(i == 0)
    def _():
        kb_ref[...] = k_ref[0].astype(BF16)
        vb_ref[...] = v_ref[0].astype(BF16)

    q = q_ref[0].astype(BF16)
    lane = lax.broadcasted_iota(jnp.int32, (tq, LANES), 1)
    sub = lax.broadcasted_iota(jnp.int32, (SUBLANES, tq), 0)
    cq, ck_grp, ck_sel = [], [], []
    for h in heads:
        head = hg * n_heads + h
        cq.append(jnp.sum(jnp.where(lane == head, c_ref[0], 0.0), axis=1, keepdims=True) * LOG2_E)
        base = pl.multiple_of((head // SUBLANES) * SUBLANES, SUBLANES)
        ck_grp.append(pl.ds(base, SUBLANES))
        ck_sel.append(sub == (head - base))
    m_ref[...] = jnp.full_like(m_ref, -jnp.inf)
    l_ref[...] = jnp.zeros_like(l_ref)
    acc_ref[...] = jnp.zeros_like(acc_ref)

    def tile(j, masked):
        ks = pl.ds(pl.multiple_of(j * tq, tq), tq)

        def logits(h):
            s = lax.dot_general(q[:, lanes[h]], kb_ref[ks, lanes[h]], NT_DIMS, preferred_element_type=F32) * scale2
            s = s + cq[h] - _select_row(ct_ref[0, ck_grp[h], ks], ck_sel[h]) * LOG2_E
            if masked:
                r = lax.broadcasted_iota(jnp.int32, (tq, tq), 0)
                cc = lax.broadcasted_iota(jnp.int32, (tq, tq), 1)
                s = jnp.where(cc <= r, s, -jnp.inf)
            return s

        def finish(h, s):
            m_old = m_ref[:, lanes[h]]
            m_new = jnp.maximum(m_old, jnp.max(s, axis=-1, keepdims=True))
            p = jnp.exp2(s - _lane_tiles(m_new, tq // LANES))
            alpha = jnp.exp2(m_old - m_new)
            pv = jnp.dot(p.astype(BF16), vb_ref[ks, lanes[h]], preferred_element_type=F32)
            l_ref[:, lanes[h]] = alpha * l_ref[:, lanes[h]] + jnp.sum(p, axis=-1, keepdims=True)
            acc_ref[:, lanes[h]] = alpha * acc_ref[:, lanes[h]] + pv
            m_ref[:, lanes[h]] = m_new

        s_prev = logits(0)
        for h in range(1, n_heads):
            s_cur = logits(h)
            finish(h - 1, s_prev)
            s_prev = s_cur
        finish(n_heads - 1, s_prev)

    def body(j, carry):
        tile(j, False)
        return carry

    lax.fori_loop(0, i, body, 0)
    tile(i, True)
    o_ref[0] = (acc_ref[...] / l_ref[...]).astype(o_ref.dtype)


def fox_attention_prompt(q, k, v, c, ct, out_width, cast_jobs=()):
    bsz, t, w = q.shape
    d = HEAD_DIM
    heads = w // d
    hb = 4 if heads % 4 == 0 else 1
    tq = _tile(t, 256)
    blk = pl.BlockSpec((1, tq, hb * d), lambda b, h, i: (b, i, h))
    full = pl.BlockSpec((1, t, hb * d), lambda b, h, i: (b, 0, h))
    grid = (bsz, heads // hb, t // tq)
    side = _side_cast_plan(cast_jobs, grid[0] * grid[1] * grid[2],
                           lambda b, h, i: (b * grid[1] + h) * grid[2] + i)
    outs = pl.pallas_call(
        functools.partial(_fox_prompt_body, tq=tq, n_heads=hb, n_side=len(cast_jobs)),
        out_shape=[jax.ShapeDtypeStruct((bsz, t, out_width), BF16)] + side[2],
        grid=grid,
        in_specs=[blk, full, full,
                  pl.BlockSpec((1, tq, LANES), lambda b, h, i: (b, i, 0)),
                  pl.BlockSpec((1, LANES, t), lambda b, h, i: (b, 0, 0))] + side[0],
        out_specs=[blk] + side[1],
        scratch_shapes=[pltpu.VMEM((t, hb * d), BF16), pltpu.VMEM((t, hb * d), BF16),
                        pltpu.VMEM((tq, hb * d), F32), pltpu.VMEM((tq, hb * d), F32),
                        pltpu.VMEM((tq, hb * d), F32)],
        compiler_params=_cparams(("arbitrary", "arbitrary", "arbitrary")),
        name="fox_attention_prompt",
    )(q, k, v, c, ct, *side[3])
    return outs[0], list(outs[1:])


def _suffix_body(pt_ref, *refs, pages_per_step):
    del pt_ref
    x_refs = refs[:pages_per_step]
    d_ref, carry_ref = refs[pages_per_step:]

    @pl.when(pl.program_id(1) == 0)
    def _():
        carry_ref[...] = jnp.zeros_like(carry_ref)

    n = x_refs[0].shape[1]
    later = (lax.broadcasted_iota(jnp.int32, (n, n), 0) > lax.broadcasted_iota(jnp.int32, (n, n), 1)).astype(F32)
    ones = jnp.ones((n, n), F32)
    within = [lax.dot_general(x_ref[0], later, TN_DIMS, precision=HIGHEST, preferred_element_type=F32)
              for x_ref in x_refs]
    total = [lax.dot_general(x_ref[0], ones, TN_DIMS, precision=HIGHEST, preferred_element_type=F32)
             for x_ref in x_refs]
    carry = carry_ref[...]
    for r in range(pages_per_step):
        d_ref[0, pages_per_step - 1 - r] = within[r] + carry
        carry = carry + total[r]
    carry_ref[...] = carry


def suffix_log_forget(cache_logf, page_table):
    bsz, n_pages = page_table.shape
    _, page, heads = cache_logf.shape
    r_pages = 16 if n_pages % 16 == 0 else 1
    steps = n_pages // r_pages

    def page_spec(r):
        return pl.BlockSpec((1, page, heads), lambda b, p, pt: (pt[b, n_pages - 1 - (p * r_pages + r)], 0, 0))

    grid_spec = pltpu.PrefetchScalarGridSpec(
        num_scalar_prefetch=1,
        grid=(bsz, steps),
        in_specs=[page_spec(r) for r in range(r_pages)],
        out_specs=pl.BlockSpec((1, r_pages, heads, page), lambda b, p, pt: (b, steps - 1 - p, 0, 0)),
        scratch_shapes=[pltpu.VMEM((heads, page), F32)],
    )
    return pl.pallas_call(
        functools.partial(_suffix_body, pages_per_step=r_pages),
        out_shape=jax.ShapeDtypeStruct((bsz, n_pages, heads, page), F32),
        grid_spec=grid_spec,
        compiler_params=_cparams(("parallel", "arbitrary")),
        name="suffix_log_forget",
    )(page_table, *([cache_logf] * r_pages))


def _fox_decode_body(pt_ref, q_ref, *rest, n_heads, pages_per_step, page):
    del pt_ref
    r_pages = pages_per_step
    k_refs = rest[:r_pages]
    v_refs = rest[r_pages:2 * r_pages]
    (d_ref, cn_ref, cnt_ref, kn_ref, vn_ref, o_ref, m_ref, l_ref, acc_ref) = rest[2 * r_pages:]
    p_idx = pl.program_id(1)
    d = HEAD_DIM
    tq = q_ref.shape[1]
    scale = d ** -0.5
    heads = range(n_heads)
    lanes = [slice(h * d, (h + 1) * d) for h in heads]

    @pl.when(p_idx == 0)
    def _():
        m_ref[...] = jnp.full_like(m_ref, -jnp.inf)
        l_ref[...] = jnp.zeros_like(l_ref)
        acc_ref[...] = jnp.zeros_like(acc_ref)

    def update(s, v_tiles):
        m_old = [m_ref[:, lanes[h]] for h in heads]
        m_new = [jnp.maximum(m_old[h], jnp.max(s[h], axis=-1, keepdims=True)) for h in heads]
        p = [jnp.exp(s[h] - _lane_tiles(m_new[h], s[h].shape[1] // LANES)) for h in heads]
        alpha = [jnp.exp(m_old[h] - m_new[h]) for h in heads]
        pb = [x.astype(BF16) for x in p]
        pv = []
        for h in heads:
            acc = jnp.dot(pb[h][:, 0:d], v_tiles[h][0], preferred_element_type=F32)
            for r in range(1, len(v_tiles[h])):
                acc = acc + jnp.dot(pb[h][:, r * d:(r + 1) * d], v_tiles[h][r], preferred_element_type=F32)
            pv.append(acc)
        for h in heads:
            l_ref[:, lanes[h]] = alpha[h] * l_ref[:, lanes[h]] + jnp.sum(p[h], axis=-1, keepdims=True)
            acc_ref[:, lanes[h]] = alpha[h] * acc_ref[:, lanes[h]] + pv[h]
            m_ref[:, lanes[h]] = m_new[h]

    qb = q_ref[0].astype(BF16)

    def head_rows(ref, h):
        return ref[0, pl.ds(h, page, stride=n_heads), :].astype(BF16)

    s = []
    for h in heads:
        parts = [lax.dot_general(qb[:, lanes[h]], head_rows(k_refs[r], h), NT_DIMS, preferred_element_type=F32) * scale
                 + (cn_ref[0, :, h:h + 1] + d_ref[0, r, h:h + 1, :]) for r in range(r_pages)]
        s.append(parts[0] if r_pages == 1 else jnp.concatenate(parts, axis=1))
    update(s, [[head_rows(v_refs[r], h) for r in range(r_pages)] for h in heads])

    @pl.when(p_idx == pl.num_programs(1) - 1)
    def _():
        n_new = kn_ref.shape[1]
        r_i = lax.broadcasted_iota(jnp.int32, (tq, n_new), 0)
        c_i = lax.broadcasted_iota(jnp.int32, (tq, n_new), 1)
        s_new = [lax.dot_general(qb[:, lanes[h]], kn_ref[0, :, lanes[h]].astype(BF16), NT_DIMS,
                                 preferred_element_type=F32) * scale
                 + (cn_ref[0, :, h:h + 1] - cnt_ref[0, h:h + 1, :]) for h in heads]
        s_new = [jnp.where(c_i <= r_i, x, -jnp.inf) for x in s_new]
        update(s_new, [[vn_ref[0, :, lanes[h]].astype(BF16)] for h in heads])
        o_ref[0] = (acc_ref[...] / l_ref[...]).astype(o_ref.dtype)


def fox_attention_decode(q, cache_k, cache_v, d_suffix, cn, cnt, k_new, v_new, page_table, out_width):
    bsz, tq, w = q.shape
    heads = w // HEAD_DIM
    rows = cache_k.shape[1]
    page = rows // heads
    n_pages = page_table.shape[1]
    tn = k_new.shape[1]
    r_pages = 4 if n_pages % 4 == 0 else 1
    steps = n_pages // r_pages

    def page_spec(r):
        return pl.BlockSpec((1, rows, HEAD_DIM), lambda b, p, pt: (pt[b, p * r_pages + r], 0, 0))

    whole = lambda shape: pl.BlockSpec((1,) + shape, lambda b, p, pt: (b, 0, 0))
    grid_spec = pltpu.PrefetchScalarGridSpec(
        num_scalar_prefetch=1,
        grid=(bsz, steps),
        in_specs=[whole((tq, w))] + [page_spec(r) for r in range(r_pages)] * 2
                 + [pl.BlockSpec((1, r_pages, heads, page), lambda b, p, pt: (b, p, 0, 0)),
                    pl.BlockSpec((1, tq, LANES), lambda b, p, pt: (b, 0, 0)),
                    whole((LANES, tn)), whole((tn, w)), whole((tn, w))],
        out_specs=pl.BlockSpec((1, tq, w), lambda b, p, pt: (b, 0, 0)),
        scratch_shapes=[pltpu.VMEM((tq, w), F32), pltpu.VMEM((tq, w), F32), pltpu.VMEM((tq, w), F32)],
    )
    return pl.pallas_call(
        functools.partial(_fox_decode_body, n_heads=heads, pages_per_step=r_pages, page=page),
        out_shape=jax.ShapeDtypeStruct((bsz, tq, out_width), BF16),
        grid_spec=grid_spec,
        compiler_params=_cparams(("parallel", "arbitrary")),
        name="fox_attention_decode",
    )(page_table, q, *([cache_k] * r_pages), *([cache_v] * r_pages), d_suffix, cn, cnt, k_new, v_new)


def _pad_rows(x, rows):
    pad = [(0, 0)] * x.ndim
    pad[1] = (0, rows - x.shape[1])
    return jnp.pad(x, pad)


def _trunk(hp, hs, dims_p, dims_s, mem_p, mem_s, hg_s0, decode, wts):
    d_model = hp.shape[1]
    tok_w = wts["tok_w"]
    mem_w = d_model - tok_w
    mem_heads = mem_p[0].shape[3]
    to4 = lambda a: a.reshape(a.shape[:3] + (mem_w,))
    mem_p = (to4(mem_p[0]), to4(mem_p[1]))
    mem_s = (to4(mem_s[0]), to4(mem_s[1]))
    n_a = wts["w_in_a"].shape[0]
    depth = wts["w_out"].shape[0]
    (bp, tp), (bs, ts) = dims_p, dims_s
    states_p, states_s = [], []
    hn_p = rmsnorm_bf16(hp, wts["norm_pre_mix"][0])
    hn_s = rmsnorm_bf16(hs, wts["norm_pre_mix"][0])
    kv_p = kv_s = None
    hkv_p = hkv_s = None
    for layer in range(depth):
        cast_jobs = [(wts["w_out"], layer), (wts["w_mlp_up"], layer), (wts["w_mlp_down"], layer)]
        if layer == n_a:
            wkv = wts["w_kv"]
            k_p, k_s = matmul(hkv_p, hkv_s, wkv, 0, tok_w)
            v_p, v_s = matmul(hkv_p, hkv_s, wkv, tok_w, tok_w)
            lf_p, lf_s = matmul(hkv_p, hkv_s, wts["w_fg"], 0, LANES, kind="logsig", vecs=(wts["b_fg"],))
            kv_p, kv_s = (k_p, v_p, lf_p), (k_s, v_s, lf_s)
        if layer < n_a:
            w_in = wts["w_in_a"]
            q = matmul(hn_p, hn_s, w_in, 0, tok_w, kind="silu", layer=layer)
            f = matmul(hn_p, hn_s, w_in, tok_w, tok_w, kind="fgate", vecs=(wts["lb"][layer],), layer=layer)
            v = matmul(hn_p, hn_s, w_in, 2 * tok_w, tok_w, layer=layer)
            g = matmul(hn_p, hn_s, w_in, 3 * tok_w, tok_w, kind="silu", layer=layer)
            zm_p, zm_s = matmul(hn_p, hn_s, w_in, 4 * tok_w, mem_w, layer=layer)
            gain = wts["hg_out_norm"][layer]
            shp_p, shp_s = (bp, tp, tok_w), (bs, ts, tok_w)
            o_p, st_p, w_bf = hgrn2(q[0].reshape(shp_p), f[0].reshape(shp_p), v[0].reshape(shp_p),
                                    g[0].reshape(shp_p), gain, None, d_model, cast_jobs=cast_jobs)
            o_s, st_s, _ = hgrn2(q[1].reshape(shp_s), f[1].reshape(shp_s), v[1].reshape(shp_s), g[1].reshape(shp_s),
                                 gain, hg_s0[layer], d_model, wts["t_valid"])
            states_p.append(st_p)
            states_s.append(st_s)
        else:
            w_in = wts["w_in_b"]
            q_p, q_s = matmul(hn_p, hn_s, w_in, 0, tok_w, layer=layer - n_a)
            zm_p, zm_s = matmul(hn_p, hn_s, w_in, tok_w, mem_w, layer=layer - n_a)
            c, ct = cumsum_tokens(kv_p[2].reshape(bp, tp, LANES))
            o_p, w_bf = fox_attention_prompt(q_p.reshape(bp, tp, tok_w), kv_p[0].reshape(bp, tp, tok_w),
                                             kv_p[1].reshape(bp, tp, tok_w), c, ct, d_model, cast_jobs=cast_jobs)
            cn, cnt = cumsum_tokens(_pad_rows(kv_s[2].reshape(bs, ts, LANES), LANES))
            kn = _pad_rows(kv_s[0].reshape(bs, ts, tok_w), LANES)
            vn = _pad_rows(kv_s[1].reshape(bs, ts, tok_w), LANES)
            o_s = fox_attention_decode(q_s.reshape(bs, ts, tok_w), decode["cache_k"], decode["cache_v"],
                                       decode["d_suffix"], cn, cnt, kn, vn, decode["page_table"], d_model)
        o_p = memory_attention(zm_p.reshape(bp, tp, mem_w), mem_p[0], mem_p[1], layer, mem_heads, o_p, tok_w)
        o_s = memory_attention(zm_s.reshape(bs, ts, mem_w), mem_s[0], mem_s[1], layer, mem_heads, o_s, tok_w)
        w_out_b, w_up_b, w_down_b = [w[None] for w in w_bf]
        mixed_p, mixed_s = matmul(o_p.reshape(bp * tp, d_model), o_s.reshape(bs * ts, d_model),
                                  w_out_b, 0, d_model)
        hp, (hn_p,) = post_norm_residual(mixed_p, hp, wts["norm_post_mix"][layer], [wts["norm_pre_mlp"][layer]])
        hs, (hn_s,) = post_norm_residual(mixed_s, hs, wts["norm_post_mix"][layer], [wts["norm_pre_mlp"][layer]])
        u_p, u_s = matmul(hn_p, hn_s, w_up_b, 0, w_up_b.shape[2], kind="relu2", out_dtype=BF16)
        down_p, down_s = matmul_long_k(u_p, u_s, w_down_b, 0)
        nxt = []
        if layer + 1 < depth:
            nxt = [wts["norm_pre_mix"][layer + 1]] + ([wts["kv_norm"]] if layer + 1 == n_a else [])
        hp, hns_p = post_norm_residual(down_p, hp, wts["norm_post_mlp"][layer], nxt)
        hs, hns_s = post_norm_residual(down_s, hs, wts["norm_post_mlp"][layer], nxt)
        if nxt:
            hn_p, hn_s = hns_p[0], hns_s[0]
            if layer + 1 == n_a:
                hkv_p, hkv_s = hns_p[1], hns_s[1]
    return (hp, jnp.stack(states_p), kv_p), (hs, jnp.stack(states_s), kv_s)


def kernel(x_prompt, x_sample, state_hgrn, cache_k, cache_v, cache_logf, cache_mem_k, cache_mem_v, page_table, mem_prompt, norm_pre_mix, norm_post_mix, norm_pre_mlp, norm_post_mlp, w_in_a, hg_lb_logits, hg_out_norm, w_in_b, kv_norm, w_kv_shared, b_fgate, w_out, mem_norm, w_mem_kv, w_mlp_up, w_mlp_down):
    bp, tp, d_model = x_prompt.shape
    bs, ts, _ = x_sample.shape
    fox_heads, fox_dim = cache_k.shape[2], cache_k.shape[3]
    assert fox_dim == HEAD_DIM and state_hgrn.shape[-1] == HEAD_DIM
    tok_w = fox_heads * fox_dim
    mem_w = d_model - tok_w
    mem_len = mem_prompt.shape[1]
    depth = w_out.shape[0]
    n_pool, page = cache_k.shape[0], cache_k.shape[1]
    t_pad = max(SUBLANES, ts)

    lb_all = jnp.cumsum(jax.nn.softmax(hg_lb_logits.astype(F32), axis=0), axis=0)
    wts = {
        "tok_w": tok_w, "t_valid": ts,
        "norm_pre_mix": norm_pre_mix, "norm_post_mix": norm_post_mix,
        "norm_pre_mlp": norm_pre_mlp, "norm_post_mlp": norm_post_mlp,
        "kv_norm": kv_norm, "hg_out_norm": hg_out_norm, "lb": lb_all,
        "w_in_a": w_in_a, "w_in_b": w_in_b,
        "w_kv": w_kv_shared[None],
        "w_fg": jnp.pad(w_kv_shared[:, 2 * tok_w:], ((0, 0), (0, LANES - fox_heads)))[None],
        "b_fg": jnp.pad(b_fgate.astype(F32), (0, LANES - fox_heads)),
        "w_out": w_out, "w_mlp_up": w_mlp_up, "w_mlp_down": w_mlp_down,
    }

    mem_n = rmsnorm_bf16(mem_prompt.reshape(bp * mem_len, d_model), mem_norm)
    mem_k_flat = matmul(mem_n, None, w_mem_kv, 0, mem_w, all_layers=True)
    mem_v_flat = matmul(mem_n, None, w_mem_kv, mem_w, mem_w, all_layers=True)
    mem_heads = cache_mem_k.shape[3]
    mem_shape = (depth, bp, mem_len, mem_heads, mem_w // mem_heads)
    mem_k_prompt = mem_k_flat.reshape(mem_shape)
    mem_v_prompt = mem_v_flat.reshape(mem_shape)

    decode = {
        "cache_k": cache_k.reshape(n_pool, page * fox_heads, fox_dim),
        "cache_v": cache_v.reshape(n_pool, page * fox_heads, fox_dim),
        "d_suffix": suffix_log_forget(cache_logf, page_table),
        "page_table": page_table,
    }
    xs = _pad_rows(x_sample, t_pad).reshape(bs * t_pad, d_model)
    (y_p, hg_p, (k_p, v_p, lf_p)), (y_s, hg_s, (k_s, v_s, lf_s)) = _trunk(
        x_prompt.reshape(bp * tp, d_model), xs, (bp, tp), (bs, t_pad),
        (mem_k_prompt, mem_v_prompt), (cache_mem_k, cache_mem_v), state_hgrn, decode, wts)

    def unpad(a, width):
        return a.reshape(bs, t_pad, width)[:, :ts]

    return (y_p.reshape(bp, tp, d_model),
            unpad(y_s, d_model),
            hg_p.astype(state_hgrn.dtype),
            hg_s.astype(state_hgrn.dtype),
            k_p.reshape(bp, tp, fox_heads, fox_dim),
            v_p.reshape(bp, tp, fox_heads, fox_dim),
            lf_p[:, :fox_heads].reshape(bp, tp, fox_heads),
            unpad(k_s, tok_w).reshape(bs, ts, fox_heads, fox_dim),
            unpad(v_s, tok_w).reshape(bs, ts, fox_heads, fox_dim),
            unpad(lf_s, LANES)[:, :, :fox_heads],
            mem_k_prompt, mem_v_prompt)
```

```python
import functools

import jax
import jax.numpy as jnp
from jax import lax
from jax.experimental import pallas as pl
from jax.experimental.pallas import tpu as pltpu

F32 = jnp.float32
BF16 = jnp.bfloat16

NORM_EPS = 1e-6
HEAD_DIM = 128
LANES = 128
SUBLANES = 8
VMEM_LIMIT = 52 * 1024 * 1024
HG_CHUNK = 64
HG_SAFE_DECAY = 60.0
HIGHEST = lax.Precision.HIGHEST
LOG2_E = 1.4426950408889634

NT_DIMS = (((1,), (1,)), ((), ()))
TN_DIMS = (((0,), (0,)), ((), ()))


def _cparams(semantics):
    return pltpu.CompilerParams(dimension_semantics=semantics, vmem_limit_bytes=VMEM_LIMIT)


def _tile(n, want):
    if n <= want:
        return n
    t = want
    while t >= 16:
        if n % t == 0:
            return t
        t -= 16
    return n


def _lane_tiles(x, n):
    return x if n == 1 else jnp.concatenate([x] * n, axis=1)


def _cumsum_rows(tri, x):
    w = x.shape[1]
    hi = x.astype(BF16)
    rest = x - hi.astype(F32)
    mid = rest.astype(BF16)
    lo = (rest - mid.astype(F32)).astype(BF16)
    y = jnp.dot(tri, jnp.concatenate([hi, mid, lo], axis=1), preferred_element_type=F32)
    return y[:, :w] + y[:, w:2 * w] + y[:, 2 * w:]


def _select_row(group, sel):
    return jnp.sum(jnp.where(sel, group, 0.0), axis=0, keepdims=True)


def _rms_scale(x):
    return lax.rsqrt(jnp.mean(x * x, axis=-1, keepdims=True) + NORM_EPS)


def _rmsnorm_body(x_ref, g_ref, o_ref):
    x = x_ref[...]
    o_ref[...] = (x * _rms_scale(x) * g_ref[...]).astype(o_ref.dtype)


def rmsnorm_bf16(x, gain):
    m, d = x.shape
    tm = _tile(m, 256)
    return pl.pallas_call(
        _rmsnorm_body,
        out_shape=jax.ShapeDtypeStruct((m, d), BF16),
        grid=(m // tm,),
        in_specs=[pl.BlockSpec((tm, d), lambda i: (i, 0)), pl.BlockSpec((1, d), lambda i: (0, 0))],
        out_specs=pl.BlockSpec((tm, d), lambda i: (i, 0)),
        compiler_params=_cparams(("parallel",)),
        name="rmsnorm",
    )(x, gain.reshape(1, d))


def _post_body(m_ref, h_ref, gp_ref, *rest, n_next):
    g_next = rest[:n_next]
    ho_ref = rest[n_next]
    hn_refs = rest[n_next + 1:]
    m = m_ref[...]
    h = h_ref[...] + m * _rms_scale(m) * gp_ref[...]
    ho_ref[...] = h
    if n_next:
        hs = h * _rms_scale(h)
        for g_ref, hn_ref in zip(g_next, hn_refs):
            hn_ref[...] = (hs * g_ref[...]).astype(hn_ref.dtype)


def post_norm_residual(mixed, h, gain_post, next_gains):
    m, d = h.shape
    tm = _tile(m, 256)
    n_next = len(next_gains)
    row = pl.BlockSpec((tm, d), lambda i: (i, 0))
    vec = pl.BlockSpec((1, d), lambda i: (0, 0))
    outs = pl.pallas_call(
        functools.partial(_post_body, n_next=n_next),
        out_shape=[jax.ShapeDtypeStruct((m, d), F32)] + [jax.ShapeDtypeStruct((m, d), BF16)] * n_next,
        grid=(m // tm,),
        in_specs=[row, row, vec] + [vec] * n_next,
        out_specs=[row] * (1 + n_next),
        compiler_params=_cparams(("parallel",)),
        name="post_norm_residual",
    )(mixed, h, gain_post.reshape(1, d), *[g.reshape(1, d) for g in next_gains])
    return outs[0], list(outs[1:])


def _epilogue(kind, acc, vecs):
    if kind == "none":
        return acc
    if kind == "silu":
        return acc * jax.nn.sigmoid(acc)
    if kind == "fgate":
        lb = vecs[0][...]
        return lb + (1.0 - lb) * jax.nn.sigmoid(acc)
    if kind == "relu2":
        r = jnp.maximum(acc, 0.0)
        return r * r
    if kind == "logsig":
        return jax.nn.log_sigmoid(acc + vecs[0][...])
    raise ValueError(kind)


def _mm_body(*refs, kind, n_vec, has_second, cast_w, trans_w):
    if has_second:
        x_ref, x2_ref, w_ref = refs[:3]
        rest = refs[3:]
    else:
        x_ref, w_ref = refs[:2]
        x2_ref = None
        rest = refs[2:]
    vecs = rest[:n_vec]
    outs = rest[n_vec:]
    o_ref = outs[0]
    i = pl.program_id(2)
    if cast_w:
        wb_ref = outs[-1]

        @pl.when(i == 0)
        def _():
            wb_ref[...] = w_ref[...].astype(BF16)
    else:
        wb_ref = w_ref

    def mm(a_ref):
        if trans_w:
            return lax.dot_general(a_ref[...], wb_ref[...], NT_DIMS, preferred_element_type=F32)
        return jnp.dot(a_ref[...], wb_ref[...], preferred_element_type=F32)

    o_ref[...] = _epilogue(kind, mm(x_ref), vecs).astype(o_ref.dtype)

    if has_second:
        o2_ref = outs[1]

        @pl.when(i == pl.num_programs(2) - 1)
        def _():
            o2_ref[...] = _epilogue(kind, mm(x2_ref), vecs).astype(o2_ref.dtype)


def matmul(x, x2, w, col_off, n_cols, kind="none", vecs=(), out_dtype=F32, layer=0, all_layers=False,
           trans_w=False):
    m, k = x.shape
    nl = w.shape[0] if all_layers else 1
    cast_w = w.dtype != BF16
    tm = _tile(m, 1024)
    tn = _tile(n_cols, 512 if cast_w else 1024)
    assert col_off % tn == 0 and n_cols % tn == 0
    jo = col_off // tn
    has_second = x2 is not None
    vec_arrs = [v.reshape(1, n_cols).astype(F32) for v in vecs]
    in_specs = [pl.BlockSpec((tm, k), lambda l, j, i: (i, 0))]
    args = [x]
    out_shape = [jax.ShapeDtypeStruct((nl, m, n_cols), out_dtype)]
    out_specs = [pl.BlockSpec((None, tm, tn), lambda l, j, i: (l, i, j))]
    if has_second:
        m2 = x2.shape[0]
        in_specs.append(pl.BlockSpec((m2, k), lambda l, j, i: (0, 0)))
        args.append(x2)
        out_shape.append(jax.ShapeDtypeStruct((nl, m2, n_cols), out_dtype))
        out_specs.append(pl.BlockSpec((None, m2, tn), lambda l, j, i: (l, 0, j)))
    w_layer = lambda l: l if all_layers else layer
    if trans_w:
        w_block = (tn, k)
        in_specs.append(pl.BlockSpec((None, tn, k), lambda l, j, i: (w_layer(l), jo + j, 0)))
    else:
        w_block = (k, tn)
        in_specs.append(pl.BlockSpec((None, k, tn), lambda l, j, i: (w_layer(l), 0, jo + j)))
    in_specs += [pl.BlockSpec((1, tn), lambda l, j, i: (0, j))] * len(vec_arrs)
    outs = pl.pallas_call(
        functools.partial(_mm_body, kind=kind, n_vec=len(vec_arrs), has_second=has_second, cast_w=cast_w,
                          trans_w=trans_w),
        out_shape=out_shape,
        grid=(nl, n_cols // tn, m // tm),
        in_specs=in_specs,
        out_specs=out_specs,
        scratch_shapes=[pltpu.VMEM(w_block, BF16)] if cast_w else [],
        compiler_params=_cparams(("parallel", "parallel", "arbitrary")),
        name="matmul_" + kind,
    )(*args, w, *vec_arrs)
    if all_layers:
        return outs[0]
    return (outs[0][0], outs[1][0]) if has_second else outs[0][0]


def _mm_acc_body(x_ref, x2_ref, w_ref, o_ref, o2_ref):
    i = pl.program_id(1)
    kk = pl.program_id(2)

    @pl.when(kk == 0)
    def _():
        o_ref[...] = jnp.zeros_like(o_ref)

    o_ref[...] += jnp.dot(x_ref[...], w_ref[...], preferred_element_type=F32)

    first = i == 0

    @pl.when(jnp.logical_and(first, kk == 0))
    def _():
        o2_ref[...] = jnp.zeros_like(o2_ref)

    @pl.when(first)
    def _():
        o2_ref[...] += jnp.dot(x2_ref[...], w_ref[...], preferred_element_type=F32)


def matmul_long_k(x, x2, w, layer):
    m, k = x.shape
    m2 = x2.shape[0]
    n = w.shape[2]
    tm = _tile(m, 1024)
    tn = _tile(n, 1024)
    tk = _tile(k, 4096)
    return pl.pallas_call(
        _mm_acc_body,
        out_shape=[jax.ShapeDtypeStruct((m, n), F32), jax.ShapeDtypeStruct((m2, n), F32)],
        grid=(n // tn, m // tm, k // tk),
        in_specs=[pl.BlockSpec((tm, tk), lambda j, i, kk: (i, kk)),
                  pl.BlockSpec((m2, tk), lambda j, i, kk: (0, kk)),
                  pl.BlockSpec((None, tk, tn), lambda j, i, kk: (layer, kk, j))],
        out_specs=[pl.BlockSpec((tm, tn), lambda j, i, kk: (i, j)),
                   pl.BlockSpec((m2, tn), lambda j, i, kk: (0, j))],
        compiler_params=_cparams(("parallel", "arbitrary", "arbitrary")),
        name="matmul_long_k",
    )(x, x2, w)


def _side_cast_plan(jobs, n_steps, step_of):
    in_specs, out_specs, out_shapes, arrays = [], [], [], []
    for w, layer in jobs:
        _, k, c = w.shape
        rows = next(r for r in range(16, k + 1, 16) if k % r == 0 and k // r <= n_steps)
        last = k // rows - 1
        in_specs.append(pl.BlockSpec(
            (None, rows, c), lambda *g, layer=layer, last=last: (layer, jnp.minimum(step_of(*g), last), 0)))
        out_specs.append(pl.BlockSpec((rows, c), lambda *g, last=last: (jnp.minimum(step_of(*g), last), 0)))
        out_shapes.append(jax.ShapeDtypeStruct((k, c), BF16))
        arrays.append(w)
    return in_specs, out_specs, out_shapes, arrays


def _side_cast(src_refs, dst_refs):
    for src, dst in zip(src_refs, dst_refs):
        dst[...] = src[...].astype(BF16)


def _hgrn_body(*refs, n_heads, t_chunk, t_valid, has_s0, n_side):
    q_ref, f_ref, v_ref, g_ref, gain_ref = refs[:5]
    refs = refs[5:]
    s0_ref = None
    if has_s0:
        s0_ref, refs = refs[0], refs[1:]
    side_in, refs = refs[:n_side], refs[n_side:]
    o_ref, sout_ref = refs[:2]
    side_out = refs[2:2 + n_side]
    st_ref, o_scr, lg_scr = refs[2 + n_side:]
    _side_cast(side_in, side_out)
    t_idx = pl.program_id(2)
    d = HEAD_DIM
    c = min(HG_CHUNK, t_chunk)
    n_sub = t_chunk // c
    heads = range(n_heads)
    lanes = [slice(h * d, (h + 1) * d) for h in heads]

    @pl.when(t_idx == 0)
    def _():
        for h in heads:
            if has_s0:
                st_ref[h] = s0_ref[0, h].T
            else:
                st_ref[h] = jnp.zeros((d, d), F32)

    if t_valid is not None:
        assert n_sub == 1 and t_valid <= t_chunk
        o_scr[...] = jnp.zeros_like(o_scr)
        sts = [st_ref[h] for h in heads]
        for t in range(t_valid):
            f_t = f_ref[0, t:t + 1, :]
            v_t = v_ref[0, t:t + 1, :]
            q_t = q_ref[0, t:t + 1, :]
            v_cols = [jnp.broadcast_to(v_t[:, lanes[h]], (d, d)).T for h in heads]
            sts = [sts[h] * f_t[:, lanes[h]] + v_cols[h] * (1.0 - f_t[:, lanes[h]]) for h in heads]
            q8s = [jnp.broadcast_to(q_t[:, lanes[h]], (SUBLANES, d)).astype(BF16) for h in heads]
            o8s = [lax.dot_general(q8s[h], sts[h].astype(BF16), NT_DIMS, preferred_element_type=F32) for h in heads]
            for h in heads:
                o_scr[t:t + 1, lanes[h]] = o8s[h][0:1, :]
        for h in heads:
            st_ref[h] = sts[h]

    def token_steps(h, first, count):
        hs = pl.ds(h * d, d)

        def step(t, st):
            base = pl.multiple_of(((first + t) // SUBLANES) * SUBLANES, SUBLANES)
            grp = pl.ds(base, SUBLANES)
            sel = lax.broadcasted_iota(jnp.int32, (SUBLANES, d), 0) == (first + t - base)
            f_t = _select_row(f_ref[0, grp, hs], sel)
            v_col = jnp.broadcast_to(_select_row(v_ref[0, grp, hs], sel), (d, d)).T
            st = st * f_t + v_col * (1.0 - f_t)
            q8 = jnp.broadcast_to(_select_row(q_ref[0, grp, hs], sel), (SUBLANES, d)).astype(BF16)
            o8 = lax.dot_general(q8, st.astype(BF16), NT_DIMS, preferred_element_type=F32)
            o_scr[grp, hs] = jnp.where(sel, o8, o_scr[grp, hs])
            return st

        st_ref[h] = lax.fori_loop(0, count, step, st_ref[h])

    row = lax.broadcasted_iota(jnp.int32, (c, c), 0)
    col = lax.broadcasted_iota(jnp.int32, (c, c), 1)
    causal = row >= col
    tri = causal.astype(BF16)

    def chunk_steps(sc):
        rows = pl.ds(sc * c, c)
        b = _cumsum_rows(tri, lg_scr[rows, :])
        e_last = jnp.exp(b[c - 1:c, :])
        qt = (q_ref[0, rows, :] * jnp.exp(b)).astype(BF16)
        kt = (1.0 - f_ref[0, rows, :]) * jnp.exp(-b)
        kd = (kt * e_last).astype(BF16)
        kt = kt.astype(BF16)
        vb = v_ref[0, rows, :].astype(BF16)
        stb = [st_ref[h].astype(BF16) for h in heads]
        att = [lax.dot_general(qt[:, lanes[h]], kt[:, lanes[h]], NT_DIMS, preferred_element_type=F32) for h in heads]
        o_inter = [lax.dot_general(qt[:, lanes[h]], stb[h], NT_DIMS, preferred_element_type=F32) for h in heads]
        upd = [lax.dot_general(vb[:, lanes[h]], kd[:, lanes[h]], TN_DIMS, preferred_element_type=F32) for h in heads]
        att = [jnp.where(causal, a, 0.0).astype(BF16) for a in att]
        o_intra = [jnp.dot(att[h], vb[:, lanes[h]], preferred_element_type=F32) for h in heads]
        for h in heads:
            o_scr[rows, lanes[h]] = o_intra[h] + o_inter[h]
            st_ref[h] = st_ref[h] * e_last[:, lanes[h]] + upd[h]

    if t_valid is None:
        lg_scr[...] = jnp.log(f_ref[0])
        totals = [jnp.min(jnp.sum(lg_scr[pl.ds(sc * c, c), :], axis=0, keepdims=True)) for sc in range(n_sub)]
        safe = functools.reduce(jnp.minimum, totals) > -HG_SAFE_DECAY

        @pl.when(safe)
        def _():
            for sc in range(n_sub):
                chunk_steps(sc)

        @pl.when(jnp.logical_not(safe))
        def _():
            for h in heads:
                token_steps(h, 0, t_chunk)

    for h in heads:
        o = o_scr[:, lanes[h]]
        o = o * _rms_scale(o) * gain_ref[:, lanes[h]]
        o_ref[0, :, lanes[h]] = (o * g_ref[0, :, lanes[h]]).astype(o_ref.dtype)

    @pl.when(t_idx == pl.num_programs(2) - 1)
    def _():
        for h in heads:
            sout_ref[0, h] = st_ref[h].T


def hgrn2(q, f, v, g, gain, s0, out_width, t_valid=None, cast_jobs=()):
    bsz, t, w = q.shape
    d = HEAD_DIM
    heads = w // d
    hb = 6 if heads % 6 == 0 else (4 if heads % 4 == 0 else 1)
    tc = _tile(t, 128)
    has_s0 = s0 is not None
    tok = pl.BlockSpec((1, tc, hb * d), lambda b, hg, ti: (b, ti, hg))
    in_specs = [tok, tok, tok, tok, pl.BlockSpec((1, hb * d), lambda b, hg, ti: (0, hg))]
    args = [q, f, v, g, gain.reshape(1, w)]
    state_spec = pl.BlockSpec((1, hb, d, d), lambda b, hg, ti: (b, hg, 0, 0))
    if has_s0:
        in_specs.append(state_spec)
        args.append(s0)
    grid = (bsz, heads // hb, t // tc)
    side = _side_cast_plan(cast_jobs, grid[0] * grid[1] * grid[2],
                           lambda b, hg, ti: (b * grid[1] + hg) * grid[2] + ti)
    outs = pl.pallas_call(
        functools.partial(_hgrn_body, n_heads=hb, t_chunk=tc, t_valid=t_valid, has_s0=has_s0,
                          n_side=len(cast_jobs)),
        out_shape=[jax.ShapeDtypeStruct((bsz, t, out_width), BF16),
                   jax.ShapeDtypeStruct((bsz, heads, d, d), F32)] + side[2],
        grid=grid,
        in_specs=in_specs + side[0],
        out_specs=[tok, state_spec] + side[1],
        scratch_shapes=[pltpu.VMEM((hb, d, d), F32), pltpu.VMEM((tc, hb * d), F32),
                        pltpu.VMEM((tc, hb * d), F32)],
        compiler_params=_cparams(("arbitrary", "arbitrary", "arbitrary")),
        name="hgrn2",
    )(*args, *side[3])
    return outs[0], outs[1], list(outs[2:])


def _memattn_body(q_ref, k_ref, v_ref, buf_ref, o_ref, *, n_heads, head_dim):
    del buf_ref
    scale = head_dim ** -0.5
    lanes = [slice(h * head_dim, (h + 1) * head_dim) for h in range(n_heads)]
    s = [lax.dot_general(q_ref[0, :, ls].astype(BF16), k_ref[0, :, ls].astype(BF16), NT_DIMS,
                         preferred_element_type=F32) * scale for ls in lanes]
    p = [jnp.exp(x - jnp.max(x, axis=-1, keepdims=True)) for x in s]
    p = [x / jnp.sum(x, axis=-1, keepdims=True) for x in p]
    o = [jnp.dot(x.astype(BF16), v_ref[0, :, ls].astype(BF16), preferred_element_type=F32)
         for x, ls in zip(p, lanes)]
    for x, ls in zip(o, lanes):
        o_ref[0, :, ls] = x.astype(o_ref.dtype)


def memory_attention(qm, mem_k, mem_v, layer, n_heads, o_buf, col_off):
    bsz, t, w = qm.shape
    m = mem_k.shape[2]
    tq = _tile(t, 256)
    assert col_off % w == 0
    cb = col_off // w
    kv_spec = pl.BlockSpec((None, 1, m, w), lambda b, i: (layer, b, 0, 0))
    return pl.pallas_call(
        functools.partial(_memattn_body, n_heads=n_heads, head_dim=w // n_heads),
        out_shape=jax.ShapeDtypeStruct(o_buf.shape, o_buf.dtype),
        grid=(bsz, t // tq),
        in_specs=[pl.BlockSpec((1, tq, w), lambda b, i: (b, i, 0)), kv_spec, kv_spec,
                  pl.BlockSpec(memory_space=pl.ANY)],
        out_specs=pl.BlockSpec((1, tq, w), lambda b, i: (b, i, cb)),
        input_output_aliases={3: 0},
        compiler_params=_cparams(("parallel", "parallel")),
        name="memory_attention",
    )(qm, mem_k, mem_v, o_buf)


def _cumsum_body(x_ref, c_ref, ct_ref, carry_ref):
    @pl.when(pl.program_id(1) == 0)
    def _():
        carry_ref[...] = jnp.zeros_like(carry_ref)

    n = x_ref.shape[1]
    tri = (lax.broadcasted_iota(jnp.int32, (n, n), 0) >= lax.broadcasted_iota(jnp.int32, (n, n), 1)).astype(F32)
    c = jnp.dot(tri, x_ref[0], precision=HIGHEST, preferred_element_type=F32) + carry_ref[...]
    carry_ref[...] = c[n - 1:n, :]
    c_ref[0] = c
    ct_ref[0] = c.T


def cumsum_tokens(x):
    bsz, t, w = x.shape
    tc = _tile(t, 256)
    return pl.pallas_call(
        _cumsum_body,
        out_shape=[jax.ShapeDtypeStruct((bsz, t, w), F32), jax.ShapeDtypeStruct((bsz, w, t), F32)],
        grid=(bsz, t // tc),
        in_specs=[pl.BlockSpec((1, tc, w), lambda b, i: (b, i, 0))],
        out_specs=[pl.BlockSpec((1, tc, w), lambda b, i: (b, i, 0)),
                   pl.BlockSpec((1, w, tc), lambda b, i: (b, 0, i))],
        scratch_shapes=[pltpu.VMEM((1, w), F32)],
        compiler_params=_cparams(("parallel", "arbitrary")),
        name="cumsum_tokens",
    )(x)


def _fox_prompt_body(q_ref, k_ref, v_ref, c_ref, ct_ref, *refs, tq, tk, n_heads, n_side):
    side_in, refs = refs[:n_side], refs[n_side:]
    o_ref = refs[0]
    side_out = refs[1:1 + n_side]
    kb_ref, vb_ref, m_ref, l_ref, acc_ref = refs[1 + n_side:]
    _side_cast(side_in, side_out)
    hg = pl.program_id(1)
    i = pl.program_id(2)
    d = HEAD_DIM
    scale2 = d ** -0.5 * LOG2_E
    heads = range(n_heads)
    lanes = [slice(h * d, (h + 1) * d) for h in heads]

    @pl.when(i == 0)
    def _():
        kb_ref[...] = k_ref[0].astype(BF16)
        vb_ref[...] = v_ref[0].astype(BF16)

    q = (q_ref[0] * scale2).astype(BF16)
    lane = lax.broadcasted_iota(jnp.int32, (tq, LANES), 1)
    sub = lax.broadcasted_iota(jnp.int32, (SUBLANES, tk), 0)
    cq, ck_grp, ck_sel = [], [], []
    for h in heads:
        head = hg * n_heads + h
        cq.append(jnp.sum(jnp.where(lane == head, c_ref[0], 0.0), axis=1, keepdims=True) * LOG2_E)
        base = pl.multiple_of((head // SUBLANES) * SUBLANES, SUBLANES)
        ck_grp.append(pl.ds(base, SUBLANES))
        ck_sel.append(sub == (head - base))
    m_ref[...] = jnp.full_like(m_ref, -jnp.inf)
    l_ref[...] = jnp.zeros_like(l_ref)
    acc_ref[...] = jnp.zeros_like(acc_ref)

    n_full = (i * tq) // tk

    def tile(j, masked):
        ks = pl.ds(pl.multiple_of(j * tk, tk), tk)

        def logits(h):
            s = lax.dot_general(q[:, lanes[h]], kb_ref[ks, lanes[h]], NT_DIMS, preferred_element_type=F32)
            s = s + cq[h] - _select_row(ct_ref[0, ck_grp[h], ks], ck_sel[h]) * LOG2_E
            if masked:
                r = lax.broadcasted_iota(jnp.int32, (tq, tk), 0) + (i * tq - j * tk)
                cc = lax.broadcasted_iota(jnp.int32, (tq, tk), 1)
                s = jnp.where(cc <= r, s, -jnp.inf)
            return s

        def finish(h, s):
            m_old = m_ref[:, lanes[h]]
            m_new = jnp.maximum(m_old, jnp.max(s, axis=-1, keepdims=True))
            p = jnp.exp2(s - _lane_tiles(m_new, tk // LANES))
            alpha = jnp.exp2(m_old - m_new)
            pv = jnp.dot(p.astype(BF16), vb_ref[ks, lanes[h]], preferred_element_type=F32)
            l_ref[:, lanes[h]] = alpha * l_ref[:, lanes[h]] + jnp.sum(p, axis=-1, keepdims=True)
            acc_ref[:, lanes[h]] = alpha * acc_ref[:, lanes[h]] + pv
            m_ref[:, lanes[h]] = m_new

        s_prev = logits(0)
        for h in range(1, n_heads):
            s_cur = logits(h)
            finish(h - 1, s_prev)
            s_prev = s_cur
        finish(n_heads - 1, s_prev)

    def body(j, carry):
        tile(j, False)
        return carry

    lax.fori_loop(0, n_full, body, 0)
    tile(n_full, True)
    o_ref[0] = (acc_ref[...] / l_ref[...]).astype(o_ref.dtype)


def fox_attention_prompt(q, k, v, c, ct, out_width, cast_jobs=()):
    bsz, t, w = q.shape
    d = HEAD_DIM
    heads = w // d
    hb = 4 if heads % 4 == 0 else 1
    tq = _tile(t, 256)
    tk = _tile(t, 256)
    assert tk % tq == 0
    blk = pl.BlockSpec((1, tq, hb * d), lambda b, h, i: (b, i, h))
    full = pl.BlockSpec((1, t, hb * d), lambda b, h, i: (b, 0, h))
    grid = (bsz, heads // hb, t // tq)
    side = _side_cast_plan(cast_jobs, grid[0] * grid[1] * grid[2],
                           lambda b, h, i: (b * grid[1] + h) * grid[2] + i)
    outs = pl.pallas_call(
        functools.partial(_fox_prompt_body, tq=tq, tk=tk, n_heads=hb, n_side=len(cast_jobs)),
        out_shape=[jax.ShapeDtypeStruct((bsz, t, out_width), BF16)] + side[2],
        grid=grid,
        in_specs=[blk, full, full,
                  pl.BlockSpec((1, tq, LANES), lambda b, h, i: (b, i, 0)),
                  pl.BlockSpec((1, LANES, t), lambda b, h, i: (b, 0, 0))] + side[0],
        out_specs=[blk] + side[1],
        scratch_shapes=[pltpu.VMEM((t, hb * d), BF16), pltpu.VMEM((t, hb * d), BF16),
                        pltpu.VMEM((tq, hb * d), F32), pltpu.VMEM((tq, hb * d), F32),
                        pltpu.VMEM((tq, hb * d), F32)],
        compiler_params=_cparams(("arbitrary", "arbitrary", "arbitrary")),
        name="fox_attention_prompt",
    )(q, k, v, c, ct, *side[3])
    return outs[0], list(outs[1:])


def _suffix_body(pt_ref, *refs, pages_per_step):
    del pt_ref
    x_refs = refs[:pages_per_step]
    d_ref, carry_ref = refs[pages_per_step:]

    @pl.when(pl.program_id(1) == 0)
    def _():
        carry_ref[...] = jnp.zeros_like(carry_ref)

    n = x_refs[0].shape[2]
    later = (lax.broadcasted_iota(jnp.int32, (n, n), 0) > lax.broadcasted_iota(jnp.int32, (n, n), 1)).astype(F32)
    within = [jnp.dot(x_ref[0], later, precision=HIGHEST, preferred_element_type=F32) for x_ref in x_refs]
    carry = carry_ref[...]
    for r in range(pages_per_step):
        d_ref[0, pages_per_step - 1 - r] = within[r] + carry
        carry = carry + jnp.sum(x_refs[r][0], axis=1, keepdims=True)
    carry_ref[...] = carry


def suffix_log_forget(logf_t, page_table):
    bsz, n_pages = page_table.shape
    _, heads, page = logf_t.shape
    r_pages = 16 if n_pages % 16 == 0 else 1
    steps = n_pages // r_pages

    def page_spec(r):
        return pl.BlockSpec((1, heads, page), lambda b, p, pt: (pt[b, n_pages - 1 - (p * r_pages + r)], 0, 0))

    grid_spec = pltpu.PrefetchScalarGridSpec(
        num_scalar_prefetch=1,
        grid=(bsz, steps),
        in_specs=[page_spec(r) for r in range(r_pages)],
        out_specs=pl.BlockSpec((1, r_pages, heads, page), lambda b, p, pt: (b, steps - 1 - p, 0, 0)),
        scratch_shapes=[pltpu.VMEM((heads, page), F32)],
    )
    return pl.pallas_call(
        functools.partial(_suffix_body, pages_per_step=r_pages),
        out_shape=jax.ShapeDtypeStruct((bsz, n_pages, heads, page), F32),
        grid_spec=grid_spec,
        compiler_params=_cparams(("parallel", "arbitrary")),
        name="suffix_log_forget",
    )(page_table, *([logf_t] * r_pages))


def _fox_decode_body(pt_ref, q_ref, *rest, n_heads, pages_per_step, page):
    del pt_ref
    r_pages = pages_per_step
    k_refs = rest[:r_pages]
    v_refs = rest[r_pages:2 * r_pages]
    (d_ref, cn_ref, cnt_ref, kn_ref, vn_ref, o_ref, m_ref, l_ref, acc_ref) = rest[2 * r_pages:]
    p_idx = pl.program_id(1)
    d = HEAD_DIM
    tq = q_ref.shape[1]
    scale = d ** -0.5
    heads = range(n_heads)
    lanes = [slice(h * d, (h + 1) * d) for h in heads]

    @pl.when(p_idx == 0)
    def _():
        m_ref[...] = jnp.full_like(m_ref, -jnp.inf)
        l_ref[...] = jnp.zeros_like(l_ref)
        acc_ref[...] = jnp.zeros_like(acc_ref)

    def update(s, v_tiles):
        m_old = [m_ref[:, lanes[h]] for h in heads]
        m_new = [jnp.maximum(m_old[h], jnp.max(s[h], axis=-1, keepdims=True)) for h in heads]
        p = [jnp.exp(s[h] - _lane_tiles(m_new[h], s[h].shape[1] // LANES)) for h in heads]
        alpha = [jnp.exp(m_old[h] - m_new[h]) for h in heads]
        pb = [x.astype(BF16) for x in p]
        pv = []
        for h in heads:
            acc = jnp.dot(pb[h][:, 0:d], v_tiles[h][0], preferred_element_type=F32)
            for r in range(1, len(v_tiles[h])):
                acc = acc + jnp.dot(pb[h][:, r * d:(r + 1) * d], v_tiles[h][r], preferred_element_type=F32)
            pv.append(acc)
        for h in heads:
            l_ref[:, lanes[h]] = alpha[h] * l_ref[:, lanes[h]] + jnp.sum(p[h], axis=-1, keepdims=True)
            acc_ref[:, lanes[h]] = alpha[h] * acc_ref[:, lanes[h]] + pv[h]
            m_ref[:, lanes[h]] = m_new[h]

    qb = q_ref[0].astype(BF16)

    def head_rows(ref, h):
        return ref[0, pl.ds(h, page, stride=n_heads), :].astype(BF16)

    s = []
    for h in heads:
        parts = [lax.dot_general(qb[:, lanes[h]], head_rows(k_refs[r], h), NT_DIMS, preferred_element_type=F32) * scale
                 + (cn_ref[0, :, h:h + 1] + d_ref[0, r, h:h + 1, :]) for r in range(r_pages)]
        s.append(parts[0] if r_pages == 1 else jnp.concatenate(parts, axis=1))
    update(s, [[head_rows(v_refs[r], h) for r in range(r_pages)] for h in heads])

    @pl.when(p_idx == pl.num_programs(1) - 1)
    def _():
        n_new = kn_ref.shape[1]
        r_i = lax.broadcasted_iota(jnp.int32, (tq, n_new), 0)
        c_i = lax.broadcasted_iota(jnp.int32, (tq, n_new), 1)
        s_new = [lax.dot_general(qb[:, lanes[h]], kn_ref[0, :, lanes[h]].astype(BF16), NT_DIMS,
                                 preferred_element_type=F32) * scale
                 + (cn_ref[0, :, h:h + 1] - cnt_ref[0, h:h + 1, :]) for h in heads]
        s_new = [jnp.where(c_i <= r_i, x, -jnp.inf) for x in s_new]
        update(s_new, [[vn_ref[0, :, lanes[h]].astype(BF16)] for h in heads])
        o_ref[0] = (acc_ref[...] / l_ref[...]).astype(o_ref.dtype)


def fox_attention_decode(q, cache_k, cache_v, d_suffix, cn, cnt, k_new, v_new, page_table, out_width):
    bsz, tq, w = q.shape
    heads = w // HEAD_DIM
    rows = cache_k.shape[1]
    page = rows // heads
    n_pages = page_table.shape[1]
    tn = k_new.shape[1]
    r_pages = 4 if n_pages % 4 == 0 else 1
    steps = n_pages // r_pages

    def page_spec(r):
        return pl.BlockSpec((1, rows, HEAD_DIM), lambda b, p, pt: (pt[b, p * r_pages + r], 0, 0))

    whole = lambda shape: pl.BlockSpec((1,) + shape, lambda b, p, pt: (b, 0, 0))
    grid_spec = pltpu.PrefetchScalarGridSpec(
        num_scalar_prefetch=1,
        grid=(bsz, steps),
        in_specs=[whole((tq, w))] + [page_spec(r) for r in range(r_pages)] * 2
                 + [pl.BlockSpec((1, r_pages, heads, page), lambda b, p, pt: (b, p, 0, 0)),
                    pl.BlockSpec((1, tq, LANES), lambda b, p, pt: (b, 0, 0)),
                    whole((LANES, tn)), whole((tn, w)), whole((tn, w))],
        out_specs=pl.BlockSpec((1, tq, w), lambda b, p, pt: (b, 0, 0)),
        scratch_shapes=[pltpu.VMEM((tq, w), F32), pltpu.VMEM((tq, w), F32), pltpu.VMEM((tq, w), F32)],
    )
    return pl.pallas_call(
        functools.partial(_fox_decode_body, n_heads=heads, pages_per_step=r_pages, page=page),
        out_shape=jax.ShapeDtypeStruct((bsz, tq, out_width), BF16),
        grid_spec=grid_spec,
        compiler_params=_cparams(("parallel", "arbitrary")),
        name="fox_attention_decode",
    )(page_table, q, *([cache_k] * r_pages), *([cache_v] * r_pages), d_suffix, cn, cnt, k_new, v_new)


def _pad_rows(x, rows):
    pad = [(0, 0)] * x.ndim
    pad[1] = (0, rows - x.shape[1])
    return jnp.pad(x, pad)


def _trunk(hp, hs, dims_p, dims_s, mem_p, mem_s, hg_s0, decode, wts):
    d_model = hp.shape[1]
    tok_w = wts["tok_w"]
    mem_w = d_model - tok_w
    mem_heads = mem_p[0].shape[3]
    to4 = lambda a: a.reshape(a.shape[:3] + (mem_w,))
    mem_p = (to4(mem_p[0]), to4(mem_p[1]))
    mem_s = (to4(mem_s[0]), to4(mem_s[1]))
    n_a = wts["w_in_a"].shape[0]
    depth = wts["w_out"].shape[0]
    (bp, tp), (bs, ts) = dims_p, dims_s
    states_p, states_s = [], []
    hn_p = rmsnorm_bf16(hp, wts["norm_pre_mix"][0])
    hn_s = rmsnorm_bf16(hs, wts["norm_pre_mix"][0])
    kv_p = kv_s = None
    hkv_p = hkv_s = None
    for layer in range(depth):
        cast_jobs = [(wts["w_out"], layer), (wts["w_mlp_up"], layer), (wts["w_mlp_down"], layer)]
        if layer == n_a:
            wkv = wts["w_kv"]
            k_p, k_s = matmul(hkv_p, hkv_s, wkv, 0, tok_w, trans_w=True)
            v_p, v_s = matmul(hkv_p, hkv_s, wkv, tok_w, tok_w, trans_w=True)
            lf_p, lf_s = matmul(hkv_p, hkv_s, wts["w_fg"], 0, LANES, kind="logsig", vecs=(wts["b_fg"],),
                                trans_w=True)
            kv_p, kv_s = (k_p, v_p, lf_p), (k_s, v_s, lf_s)
        if layer < n_a:
            w_in = wts["w_in_a"]
            q = matmul(hn_p, hn_s, w_in, 0, tok_w, kind="silu", layer=layer)
            f = matmul(hn_p, hn_s, w_in, tok_w, tok_w, kind="fgate", vecs=(wts["lb"][layer],), layer=layer)
            v = matmul(hn_p, hn_s, w_in, 2 * tok_w, tok_w, layer=layer)
            g = matmul(hn_p, hn_s, w_in, 3 * tok_w, tok_w, kind="silu", layer=layer)
            zm_p, zm_s = matmul(hn_p, hn_s, w_in, 4 * tok_w, mem_w, layer=layer)
            gain = wts["hg_out_norm"][layer]
            shp_p, shp_s = (bp, tp, tok_w), (bs, ts, tok_w)
            o_p, st_p, w_bf = hgrn2(q[0].reshape(shp_p), f[0].reshape(shp_p), v[0].reshape(shp_p),
                                    g[0].reshape(shp_p), gain, None, d_model, cast_jobs=cast_jobs)
            o_s, st_s, _ = hgrn2(q[1].reshape(shp_s), f[1].reshape(shp_s), v[1].reshape(shp_s), g[1].reshape(shp_s),
                                 gain, hg_s0[layer], d_model, wts["t_valid"])
            states_p.append(st_p)
            states_s.append(st_s)
        else:
            w_in = wts["w_in_b"]
            q_p, q_s = matmul(hn_p, hn_s, w_in, 0, tok_w, layer=layer - n_a)
            zm_p, zm_s = matmul(hn_p, hn_s, w_in, tok_w, mem_w, layer=layer - n_a)
            c, ct = cumsum_tokens(kv_p[2].reshape(bp, tp, LANES))
            o_p, w_bf = fox_attention_prompt(q_p.reshape(bp, tp, tok_w), kv_p[0].reshape(bp, tp, tok_w),
                                             kv_p[1].reshape(bp, tp, tok_w), c, ct, d_model, cast_jobs=cast_jobs)
            cn, cnt = cumsum_tokens(_pad_rows(kv_s[2].reshape(bs, ts, LANES), LANES))
            kn = _pad_rows(kv_s[0].reshape(bs, ts, tok_w), LANES)
            vn = _pad_rows(kv_s[1].reshape(bs, ts, tok_w), LANES)
            o_s = fox_attention_decode(q_s.reshape(bs, ts, tok_w), decode["cache_k"], decode["cache_v"],
                                       decode["d_suffix"], cn, cnt, kn, vn, decode["page_table"], d_model)
        o_p = memory_attention(zm_p.reshape(bp, tp, mem_w), mem_p[0], mem_p[1], layer, mem_heads, o_p, tok_w)
        o_s = memory_attention(zm_s.reshape(bs, ts, mem_w), mem_s[0], mem_s[1], layer, mem_heads, o_s, tok_w)
        w_out_b, w_up_b, w_down_b = [w[None] for w in w_bf]
        mixed_p, mixed_s = matmul(o_p.reshape(bp * tp, d_model), o_s.reshape(bs * ts, d_model),
                                  w_out_b, 0, d_model)
        hp, (hn_p,) = post_norm_residual(mixed_p, hp, wts["norm_post_mix"][layer], [wts["norm_pre_mlp"][layer]])
        hs, (hn_s,) = post_norm_residual(mixed_s, hs, wts["norm_post_mix"][layer], [wts["norm_pre_mlp"][layer]])
        u_p, u_s = matmul(hn_p, hn_s, w_up_b, 0, w_up_b.shape[2], kind="relu2", out_dtype=BF16)
        down_p, down_s = matmul_long_k(u_p, u_s, w_down_b, 0)
        nxt = []
        if layer + 1 < depth:
            nxt = [wts["norm_pre_mix"][layer + 1]] + ([wts["kv_norm"]] if layer + 1 == n_a else [])
        hp, hns_p = post_norm_residual(down_p, hp, wts["norm_post_mlp"][layer], nxt)
        hs, hns_s = post_norm_residual(down_s, hs, wts["norm_post_mlp"][layer], nxt)
        if nxt:
            hn_p, hn_s = hns_p[0], hns_s[0]
            if layer + 1 == n_a:
                hkv_p, hkv_s = hns_p[1], hns_s[1]
    return (hp, jnp.stack(states_p), kv_p), (hs, jnp.stack(states_s), kv_s)


def kernel(x_prompt, x_sample, state_hgrn, cache_k, cache_v, cache_logf, cache_mem_k, cache_mem_v, page_table, mem_prompt, norm_pre_mix, norm_post_mix, norm_pre_mlp, norm_post_mlp, w_in_a, hg_lb_logits, hg_out_norm, w_in_b, kv_norm, w_kv_shared, b_fgate, w_out, mem_norm, w_mem_kv, w_mlp_up, w_mlp_down):
    bp, tp, d_model = x_prompt.shape
    bs, ts, _ = x_sample.shape
    fox_heads, fox_dim = cache_k.shape[2], cache_k.shape[3]
    assert fox_dim == HEAD_DIM and state_hgrn.shape[-1] == HEAD_DIM
    tok_w = fox_heads * fox_dim
    mem_w = d_model - tok_w
    mem_len = mem_prompt.shape[1]
    depth = w_out.shape[0]
    n_pool, page = cache_k.shape[0], cache_k.shape[1]
    t_pad = max(SUBLANES, ts)

    lb_all = jnp.cumsum(jax.nn.softmax(hg_lb_logits.astype(F32), axis=0), axis=0)
    wts = {
        "tok_w": tok_w, "t_valid": ts,
        "norm_pre_mix": norm_pre_mix, "norm_post_mix": norm_post_mix,
        "norm_pre_mlp": norm_pre_mlp, "norm_post_mlp": norm_post_mlp,
        "kv_norm": kv_norm, "hg_out_norm": hg_out_norm, "lb": lb_all,
        "w_in_a": w_in_a, "w_in_b": w_in_b,
        "w_kv": w_kv_shared.T[None],
        "w_fg": jnp.pad(w_kv_shared.T[2 * tok_w:], ((0, LANES - fox_heads), (0, 0)))[None],
        "b_fg": jnp.pad(b_fgate.astype(F32), (0, LANES - fox_heads)),
        "w_out": w_out, "w_mlp_up": w_mlp_up, "w_mlp_down": w_mlp_down,
    }

    mem_n = rmsnorm_bf16(mem_prompt.reshape(bp * mem_len, d_model), mem_norm)
    mem_k_flat = matmul(mem_n, None, w_mem_kv, 0, mem_w, all_layers=True)
    mem_v_flat = matmul(mem_n, None, w_mem_kv, mem_w, mem_w, all_layers=True)
    mem_heads = cache_mem_k.shape[3]
    mem_shape = (depth, bp, mem_len, mem_heads, mem_w // mem_heads)
    mem_k_prompt = mem_k_flat.reshape(mem_shape)
    mem_v_prompt = mem_v_flat.reshape(mem_shape)

    decode = {
        "cache_k": cache_k.reshape(n_pool, page * fox_heads, fox_dim),
        "cache_v": cache_v.reshape(n_pool, page * fox_heads, fox_dim),
        "d_suffix": suffix_log_forget(cache_logf.transpose(0, 2, 1), page_table),
        "page_table": page_table,
    }
    xs = _pad_rows(x_sample, t_pad).reshape(bs * t_pad, d_model)
    (y_p, hg_p, (k_p, v_p, lf_p)), (y_s, hg_s, (k_s, v_s, lf_s)) = _trunk(
        x_prompt.reshape(bp * tp, d_model), xs, (bp, tp), (bs, t_pad),
        (mem_k_prompt, mem_v_prompt), (cache_mem_k, cache_mem_v), state_hgrn, decode, wts)

    def unpad(a, width):
        return a.reshape(bs, t_pad, width)[:, :ts]

    return (y_p.reshape(bp, tp, d_model),
            unpad(y_s, d_model),
            hg_p.astype(state_hgrn.dtype),
            hg_s.astype(state_hgrn.dtype),
            k_p.reshape(bp, tp, fox_heads, fox_dim),
            v_p.reshape(bp, tp, fox_heads, fox_dim),
            lf_p[:, :fox_heads].reshape(bp, tp, fox_heads),
            unpad(k_s, tok_w).reshape(bs, ts, fox_heads, fox_dim),
            unpad(v_s, tok_w).reshape(bs, ts, fox_heads, fox_dim),
            unpad(lf_s, LANES)[:, :, :fox_heads],
            mem_k_prompt, mem_v_prompt)
```

```python
import functools

import jax
import jax.numpy as jnp
from jax import lax
from jax.experimental import pallas as pl
from jax.experimental.pallas import tpu as pltpu

F32 = jnp.float32
BF16 = jnp.bfloat16

NORM_EPS = 1e-6
HEAD_DIM = 128
LANES = 128
SUBLANES = 8
VMEM_LIMIT = 52 * 1024 * 1024
HG_CHUNK = 64
HG_SAFE_DECAY = 60.0
HIGHEST = lax.Precision.HIGHEST
LOG2_E = 1.4426950408889634

NT_DIMS = (((1,), (1,)), ((), ()))
TN_DIMS = (((0,), (0,)), ((), ()))


def _cparams(semantics):
    return pltpu.CompilerParams(dimension_semantics=semantics, vmem_limit_bytes=VMEM_LIMIT)


def _tile(n, want):
    if n <= want:
        return n
    t = want
    while t >= 16:
        if n % t == 0:
            return t
        t -= 16
    return n


def _lane_tiles(x, n):
    return x if n == 1 else jnp.concatenate([x] * n, axis=1)


def _cumsum_rows(tri, x):
    w = x.shape[1]
    hi = x.astype(BF16)
    rest = x - hi.astype(F32)
    mid = rest.astype(BF16)
    lo = (rest - mid.astype(F32)).astype(BF16)
    y = jnp.dot(tri, jnp.concatenate([hi, mid, lo], axis=1), preferred_element_type=F32)
    return y[:, :w] + y[:, w:2 * w] + y[:, 2 * w:]


def _select_row(group, sel):
    return jnp.sum(jnp.where(sel, group, 0.0), axis=0, keepdims=True)


def _rms_scale(x):
    return lax.rsqrt(jnp.mean(x * x, axis=-1, keepdims=True) + NORM_EPS)


def _rmsnorm_body(x_ref, g_ref, o_ref):
    x = x_ref[...]
    o_ref[...] = (x * _rms_scale(x) * g_ref[...]).astype(o_ref.dtype)


def rmsnorm_bf16(x, gain):
    m, d = x.shape
    tm = _tile(m, 256)
    return pl.pallas_call(
        _rmsnorm_body,
        out_shape=jax.ShapeDtypeStruct((m, d), BF16),
        grid=(m // tm,),
        in_specs=[pl.BlockSpec((tm, d), lambda i: (i, 0)), pl.BlockSpec((1, d), lambda i: (0, 0))],
        out_specs=pl.BlockSpec((tm, d), lambda i: (i, 0)),
        compiler_params=_cparams(("parallel",)),
        name="rmsnorm",
    )(x, gain.reshape(1, d))


def _post_body(m_ref, h_ref, gp_ref, *rest, n_next):
    g_next = rest[:n_next]
    ho_ref = rest[n_next]
    hn_refs = rest[n_next + 1:]
    m = m_ref[...]
    h = h_ref[...] + m * _rms_scale(m) * gp_ref[...]
    ho_ref[...] = h
    if n_next:
        hs = h * _rms_scale(h)
        for g_ref, hn_ref in zip(g_next, hn_refs):
            hn_ref[...] = (hs * g_ref[...]).astype(hn_ref.dtype)


def post_norm_residual(mixed, h, gain_post, next_gains):
    m, d = h.shape
    tm = _tile(m, 256)
    n_next = len(next_gains)
    row = pl.BlockSpec((tm, d), lambda i: (i, 0))
    vec = pl.BlockSpec((1, d), lambda i: (0, 0))
    outs = pl.pallas_call(
        functools.partial(_post_body, n_next=n_next),
        out_shape=[jax.ShapeDtypeStruct((m, d), F32)] + [jax.ShapeDtypeStruct((m, d), BF16)] * n_next,
        grid=(m // tm,),
        in_specs=[row, row, vec] + [vec] * n_next,
        out_specs=[row] * (1 + n_next),
        compiler_params=_cparams(("parallel",)),
        name="post_norm_residual",
    )(mixed, h, gain_post.reshape(1, d), *[g.reshape(1, d) for g in next_gains])
    return outs[0], list(outs[1:])


def _epilogue(kind, acc, vecs):
    if kind == "none":
        return acc
    if kind == "silu":
        return acc * jax.nn.sigmoid(acc)
    if kind == "fgate":
        lb = vecs[0][...]
        return lb + (1.0 - lb) * jax.nn.sigmoid(acc)
    if kind == "relu2":
        r = jnp.maximum(acc, 0.0)
        return r * r
    if kind == "logsig":
        return jax.nn.log_sigmoid(acc + vecs[0][...])
    raise ValueError(kind)


def _mm_body(*refs, kind, n_vec, n_seg, has_second, cast_w, trans_w):
    x_refs, refs = refs[:n_seg], refs[n_seg:]
    x2_refs = ()
    if has_second:
        x2_refs, refs = refs[:n_seg], refs[n_seg:]
    w_ref, rest = refs[0], refs[1:]
    vecs = rest[:n_vec]
    outs = rest[n_vec:]
    o_ref = outs[0]
    i = pl.program_id(2)
    if cast_w:
        wb_ref = outs[-1]

        @pl.when(i == 0)
        def _():
            wb_ref[...] = w_ref[...].astype(BF16)
    else:
        wb_ref = w_ref

    def mm(a_refs):
        acc, off = None, 0
        for a_ref in a_refs:
            ks = a_ref.shape[1]
            if trans_w:
                part = lax.dot_general(a_ref[...], wb_ref[:, off:off + ks], NT_DIMS, preferred_element_type=F32)
            else:
                part = jnp.dot(a_ref[...], wb_ref[off:off + ks, :], preferred_element_type=F32)
            acc = part if acc is None else acc + part
            off += ks
        return acc

    o_ref[...] = _epilogue(kind, mm(x_refs), vecs).astype(o_ref.dtype)

    if has_second:
        o2_ref = outs[1]

        @pl.when(i == pl.num_programs(2) - 1)
        def _():
            o2_ref[...] = _epilogue(kind, mm(x2_refs), vecs).astype(o2_ref.dtype)


def matmul(x, x2, w, col_off, n_cols, kind="none", vecs=(), out_dtype=F32, layer=0, all_layers=False,
           trans_w=False):
    xs = x if isinstance(x, tuple) else (x,)
    m = xs[0].shape[0]
    k = sum(a.shape[1] for a in xs)
    nl = w.shape[0] if all_layers else 1
    cast_w = w.dtype != BF16
    tm = _tile(m, 1024)
    tn = _tile(n_cols, 512 if cast_w else 1024)
    assert col_off % tn == 0 and n_cols % tn == 0
    jo = col_off // tn
    has_second = x2 is not None
    vec_arrs = [v.reshape(1, n_cols).astype(F32) for v in vecs]
    in_specs = [pl.BlockSpec((tm, a.shape[1]), lambda l, j, i: (i, 0)) for a in xs]
    args = list(xs)
    out_shape = [jax.ShapeDtypeStruct((nl, m, n_cols), out_dtype)]
    out_specs = [pl.BlockSpec((None, tm, tn), lambda l, j, i: (l, i, j))]
    if has_second:
        x2s = x2 if isinstance(x2, tuple) else (x2,)
        assert [a.shape[1] for a in x2s] == [a.shape[1] for a in xs]
        m2 = x2s[0].shape[0]
        in_specs += [pl.BlockSpec((m2, a.shape[1]), lambda l, j, i: (0, 0)) for a in x2s]
        args += list(x2s)
        out_shape.append(jax.ShapeDtypeStruct((nl, m2, n_cols), out_dtype))
        out_specs.append(pl.BlockSpec((None, m2, tn), lambda l, j, i: (l, 0, j)))
    w_layer = lambda l: l if all_layers else layer
    if trans_w:
        w_block = (tn, k)
        in_specs.append(pl.BlockSpec((None, tn, k), lambda l, j, i: (w_layer(l), jo + j, 0)))
    else:
        w_block = (k, tn)
        in_specs.append(pl.BlockSpec((None, k, tn), lambda l, j, i: (w_layer(l), 0, jo + j)))
    in_specs += [pl.BlockSpec((1, tn), lambda l, j, i: (0, j))] * len(vec_arrs)
    outs = pl.pallas_call(
        functools.partial(_mm_body, kind=kind, n_vec=len(vec_arrs), n_seg=len(xs), has_second=has_second,
                          cast_w=cast_w, trans_w=trans_w),
        out_shape=out_shape,
        grid=(nl, n_cols // tn, m // tm),
        in_specs=in_specs,
        out_specs=out_specs,
        scratch_shapes=[pltpu.VMEM(w_block, BF16)] if cast_w else [],
        compiler_params=_cparams(("parallel", "parallel", "arbitrary")),
        name="matmul_" + kind,
    )(*args, w, *vec_arrs)
    if all_layers:
        return outs[0]
    return (outs[0][0], outs[1][0]) if has_second else outs[0][0]


def _mm_acc_body(x_ref, x2_ref, w_ref, o_ref, o2_ref):
    i = pl.program_id(1)
    kk = pl.program_id(2)

    @pl.when(kk == 0)
    def _():
        o_ref[...] = jnp.zeros_like(o_ref)

    o_ref[...] += jnp.dot(x_ref[...], w_ref[...], preferred_element_type=F32)

    first = i == 0

    @pl.when(jnp.logical_and(first, kk == 0))
    def _():
        o2_ref[...] = jnp.zeros_like(o2_ref)

    @pl.when(first)
    def _():
        o2_ref[...] += jnp.dot(x2_ref[...], w_ref[...], preferred_element_type=F32)


def matmul_long_k(x, x2, w, layer):
    m, k = x.shape
    m2 = x2.shape[0]
    n = w.shape[2]
    tm = _tile(m, 1024)
    tn = _tile(n, 1024)
    tk = _tile(k, 4096)
    return pl.pallas_call(
        _mm_acc_body,
        out_shape=[jax.ShapeDtypeStruct((m, n), F32), jax.ShapeDtypeStruct((m2, n), F32)],
        grid=(n // tn, m // tm, k // tk),
        in_specs=[pl.BlockSpec((tm, tk), lambda j, i, kk: (i, kk)),
                  pl.BlockSpec((m2, tk), lambda j, i, kk: (0, kk)),
                  pl.BlockSpec((None, tk, tn), lambda j, i, kk: (layer, kk, j))],
        out_specs=[pl.BlockSpec((tm, tn), lambda j, i, kk: (i, j)),
                   pl.BlockSpec((m2, tn), lambda j, i, kk: (0, j))],
        compiler_params=_cparams(("parallel", "arbitrary", "arbitrary")),
        name="matmul_long_k",
    )(x, x2, w)


def _side_cast_plan(jobs, n_steps, step_of):
    in_specs, out_specs, out_shapes, arrays = [], [], [], []
    for w, layer in jobs:
        _, k, c = w.shape
        rows = next(r for r in range(16, k + 1, 16) if k % r == 0 and k // r <= n_steps)
        last = k // rows - 1
        in_specs.append(pl.BlockSpec(
            (None, rows, c), lambda *g, layer=layer, last=last: (layer, jnp.minimum(step_of(*g), last), 0)))
        out_specs.append(pl.BlockSpec((rows, c), lambda *g, last=last: (jnp.minimum(step_of(*g), last), 0)))
        out_shapes.append(jax.ShapeDtypeStruct((k, c), BF16))
        arrays.append(w)
    return in_specs, out_specs, out_shapes, arrays


def _side_cast(src_refs, dst_refs):
    for src, dst in zip(src_refs, dst_refs):
        dst[...] = src[...].astype(BF16)


def _hgrn_body(*refs, n_heads, t_chunk, t_valid, has_s0, n_side):
    q_ref, f_ref, v_ref, g_ref, gain_ref = refs[:5]
    refs = refs[5:]
    s0_ref = None
    if has_s0:
        s0_ref, refs = refs[0], refs[1:]
    side_in, refs = refs[:n_side], refs[n_side:]
    o_ref, sout_ref = refs[:2]
    side_out = refs[2:2 + n_side]
    st_ref, o_scr, lg_scr = refs[2 + n_side:]
    _side_cast(side_in, side_out)
    t_idx = pl.program_id(2)
    d = HEAD_DIM
    c = min(HG_CHUNK, t_chunk)
    n_sub = t_chunk // c
    heads = range(n_heads)
    lanes = [slice(h * d, (h + 1) * d) for h in heads]

    @pl.when(t_idx == 0)
    def _():
        for h in heads:
            if has_s0:
                st_ref[h] = s0_ref[0, h].T
            else:
                st_ref[h] = jnp.zeros((d, d), F32)

    if t_valid is not None:
        assert n_sub == 1 and t_valid <= t_chunk
        o_scr[...] = jnp.zeros_like(o_scr)
        sts = [st_ref[h] for h in heads]
        for t in range(t_valid):
            f_t = f_ref[0, t:t + 1, :]
            v_t = v_ref[0, t:t + 1, :]
            q_t = q_ref[0, t:t + 1, :]
            v_cols = [jnp.broadcast_to(v_t[:, lanes[h]], (d, d)).T for h in heads]
            sts = [sts[h] * f_t[:, lanes[h]] + v_cols[h] * (1.0 - f_t[:, lanes[h]]) for h in heads]
            q8s = [jnp.broadcast_to(q_t[:, lanes[h]], (SUBLANES, d)).astype(BF16) for h in heads]
            o8s = [lax.dot_general(q8s[h], sts[h].astype(BF16), NT_DIMS, preferred_element_type=F32) for h in heads]
            for h in heads:
                o_scr[t:t + 1, lanes[h]] = o8s[h][0:1, :]
        for h in heads:
            st_ref[h] = sts[h]

    def token_steps(h, first, count):
        hs = pl.ds(h * d, d)

        def step(t, st):
            base = pl.multiple_of(((first + t) // SUBLANES) * SUBLANES, SUBLANES)
            grp = pl.ds(base, SUBLANES)
            sel = lax.broadcasted_iota(jnp.int32, (SUBLANES, d), 0) == (first + t - base)
            f_t = _select_row(f_ref[0, grp, hs], sel)
            v_col = jnp.broadcast_to(_select_row(v_ref[0, grp, hs], sel), (d, d)).T
            st = st * f_t + v_col * (1.0 - f_t)
            q8 = jnp.broadcast_to(_select_row(q_ref[0, grp, hs], sel), (SUBLANES, d)).astype(BF16)
            o8 = lax.dot_general(q8, st.astype(BF16), NT_DIMS, preferred_element_type=F32)
            o_scr[grp, hs] = jnp.where(sel, o8, o_scr[grp, hs])
            return st

        st_ref[h] = lax.fori_loop(0, count, step, st_ref[h])

    row = lax.broadcasted_iota(jnp.int32, (c, c), 0)
    col = lax.broadcasted_iota(jnp.int32, (c, c), 1)
    causal = row >= col
    tri = causal.astype(BF16)

    def chunk_steps(sc):
        rows = pl.ds(sc * c, c)
        b = _cumsum_rows(tri, lg_scr[rows, :])
        e_last = jnp.exp(b[c - 1:c, :])
        qt = (q_ref[0, rows, :] * jnp.exp(b)).astype(BF16)
        kt = (1.0 - f_ref[0, rows, :]) * jnp.exp(-b)
        kd = (kt * e_last).astype(BF16)
        kt = kt.astype(BF16)
        vb = v_ref[0, rows, :].astype(BF16)
        stb = [st_ref[h].astype(BF16) for h in heads]
        att = [lax.dot_general(qt[:, lanes[h]], kt[:, lanes[h]], NT_DIMS, preferred_element_type=F32) for h in heads]
        o_inter = [lax.dot_general(qt[:, lanes[h]], stb[h], NT_DIMS, preferred_element_type=F32) for h in heads]
        upd = [lax.dot_general(vb[:, lanes[h]], kd[:, lanes[h]], TN_DIMS, preferred_element_type=F32) for h in heads]
        att = [jnp.where(causal, a, 0.0).astype(BF16) for a in att]
        o_intra = [jnp.dot(att[h], vb[:, lanes[h]], preferred_element_type=F32) for h in heads]
        for h in heads:
            o_scr[rows, lanes[h]] = o_intra[h] + o_inter[h]
            st_ref[h] = st_ref[h] * e_last[:, lanes[h]] + upd[h]

    if t_valid is None:
        lg_scr[...] = jnp.log(f_ref[0])
        totals = [jnp.min(jnp.sum(lg_scr[pl.ds(sc * c, c), :], axis=0, keepdims=True)) for sc in range(n_sub)]
        safe = functools.reduce(jnp.minimum, totals) > -HG_SAFE_DECAY

        @pl.when(safe)
        def _():
            for sc in range(n_sub):
                chunk_steps(sc)

        @pl.when(jnp.logical_not(safe))
        def _():
            for h in heads:
                token_steps(h, 0, t_chunk)

    for h in heads:
        o = o_scr[:, lanes[h]]
        o = o * _rms_scale(o) * gain_ref[:, lanes[h]]
        o_ref[0, :, lanes[h]] = (o * g_ref[0, :, lanes[h]]).astype(o_ref.dtype)

    @pl.when(t_idx == pl.num_programs(2) - 1)
    def _():
        for h in heads:
            sout_ref[0, h] = st_ref[h].T


def hgrn2(q, f, v, g, gain, s0, t_valid=None, cast_jobs=()):
    bsz, t, w = q.shape
    d = HEAD_DIM
    heads = w // d
    hb = 6 if heads % 6 == 0 else (4 if heads % 4 == 0 else 1)
    tc = _tile(t, 256)
    has_s0 = s0 is not None
    tok = pl.BlockSpec((1, tc, hb * d), lambda b, hg, ti: (b, ti, hg))
    in_specs = [tok, tok, tok, tok, pl.BlockSpec((1, hb * d), lambda b, hg, ti: (0, hg))]
    args = [q, f, v, g, gain.reshape(1, w)]
    state_spec = pl.BlockSpec((1, hb, d, d), lambda b, hg, ti: (b, hg, 0, 0))
    if has_s0:
        in_specs.append(state_spec)
        args.append(s0)
    grid = (bsz, heads // hb, t // tc)
    side = _side_cast_plan(cast_jobs, grid[0] * grid[1] * grid[2],
                           lambda b, hg, ti: (b * grid[1] + hg) * grid[2] + ti)
    outs = pl.pallas_call(
        functools.partial(_hgrn_body, n_heads=hb, t_chunk=tc, t_valid=t_valid, has_s0=has_s0,
                          n_side=len(cast_jobs)),
        out_shape=[jax.ShapeDtypeStruct((bsz, t, w), BF16),
                   jax.ShapeDtypeStruct((bsz, heads, d, d), F32)] + side[2],
        grid=grid,
        in_specs=in_specs + side[0],
        out_specs=[tok, state_spec] + side[1],
        scratch_shapes=[pltpu.VMEM((hb, d, d), F32), pltpu.VMEM((tc, hb * d), F32),
                        pltpu.VMEM((tc, hb * d), F32)],
        compiler_params=_cparams(("arbitrary", "arbitrary", "arbitrary")),
        name="hgrn2",
    )(*args, *side[3])
    return outs[0], outs[1], list(outs[2:])


def _memattn_body(q_ref, k_ref, v_ref, o_ref, *, n_heads, head_dim):
    scale = head_dim ** -0.5
    lanes = [slice(h * head_dim, (h + 1) * head_dim) for h in range(n_heads)]
    s = [lax.dot_general(q_ref[0, :, ls].astype(BF16), k_ref[0, :, ls].astype(BF16), NT_DIMS,
                         preferred_element_type=F32) * scale for ls in lanes]
    p = [jnp.exp(x - jnp.max(x, axis=-1, keepdims=True)) for x in s]
    p = [x / jnp.sum(x, axis=-1, keepdims=True) for x in p]
    o = [jnp.dot(x.astype(BF16), v_ref[0, :, ls].astype(BF16), preferred_element_type=F32)
         for x, ls in zip(p, lanes)]
    for x, ls in zip(o, lanes):
        o_ref[0, :, ls] = x.astype(o_ref.dtype)


def memory_attention(qm, mem_k, mem_v, layer, n_heads):
    bsz, t, w = qm.shape
    m = mem_k.shape[2]
    tq = _tile(t, 256)
    kv_spec = pl.BlockSpec((None, 1, m, w), lambda b, i: (layer, b, 0, 0))
    blk = pl.BlockSpec((1, tq, w), lambda b, i: (b, i, 0))
    return pl.pallas_call(
        functools.partial(_memattn_body, n_heads=n_heads, head_dim=w // n_heads),
        out_shape=jax.ShapeDtypeStruct((bsz, t, w), BF16),
        grid=(bsz, t // tq),
        in_specs=[blk, kv_spec, kv_spec],
        out_specs=blk,
        compiler_params=_cparams(("parallel", "parallel")),
        name="memory_attention",
    )(qm, mem_k, mem_v)


def _cumsum_body(x_ref, c_ref, ct_ref, carry_ref):
    @pl.when(pl.program_id(1) == 0)
    def _():
        carry_ref[...] = jnp.zeros_like(carry_ref)

    n = x_ref.shape[1]
    tri = (lax.broadcasted_iota(jnp.int32, (n, n), 0) >= lax.broadcasted_iota(jnp.int32, (n, n), 1)).astype(F32)
    c = jnp.dot(tri, x_ref[0], precision=HIGHEST, preferred_element_type=F32) + carry_ref[...]
    carry_ref[...] = c[n - 1:n, :]
    c_ref[0] = c
    ct_ref[0] = c.T


def cumsum_tokens(x):
    bsz, t, w = x.shape
    tc = _tile(t, 256)
    return pl.pallas_call(
        _cumsum_body,
        out_shape=[jax.ShapeDtypeStruct((bsz, t, w), F32), jax.ShapeDtypeStruct((bsz, w, t), F32)],
        grid=(bsz, t // tc),
        in_specs=[pl.BlockSpec((1, tc, w), lambda b, i: (b, i, 0))],
        out_specs=[pl.BlockSpec((1, tc, w), lambda b, i: (b, i, 0)),
                   pl.BlockSpec((1, w, tc), lambda b, i: (b, 0, i))],
        scratch_shapes=[pltpu.VMEM((1, w), F32)],
        compiler_params=_cparams(("parallel", "arbitrary")),
        name="cumsum_tokens",
    )(x)


def _fox_prompt_body(q_ref, k_ref, v_ref, c_ref, ct_ref, *refs, tq, tk, n_heads, n_side):
    side_in, refs = refs[:n_side], refs[n_side:]
    o_ref = refs[0]
    side_out = refs[1:1 + n_side]
    kb_ref, vb_ref, m_ref, l_ref, acc_ref = refs[1 + n_side:]
    _side_cast(side_in, side_out)
    hg = pl.program_id(1)
    i = pl.program_id(2)
    d = HEAD_DIM
    scale2 = d ** -0.5 * LOG2_E
    heads = range(n_heads)
    lanes = [slice(h * d, (h + 1) * d) for h in heads]

    @pl.when(i == 0)
    def _():
        kb_ref[...] = k_ref[0].astype(BF16)
        vb_ref[...] = v_ref[0].astype(BF16)

    q = (q_ref[0] * scale2).astype(BF16)
    lane = lax.broadcasted_iota(jnp.int32, (tq, LANES), 1)
    sub = lax.broadcasted_iota(jnp.int32, (SUBLANES, tk), 0)
    cq, ck_grp, ck_sel = [], [], []
    for h in heads:
        head = hg * n_heads + h
        cq.append(jnp.sum(jnp.where(lane == head, c_ref[0], 0.0), axis=1, keepdims=True) * LOG2_E)
        base = pl.multiple_of((head // SUBLANES) * SUBLANES, SUBLANES)
        ck_grp.append(pl.ds(base, SUBLANES))
        ck_sel.append(sub == (head - base))
    m_ref[...] = jnp.full_like(m_ref, -jnp.inf)
    l_ref[...] = jnp.zeros_like(l_ref)
    acc_ref[...] = jnp.zeros_like(acc_ref)

    n_full = (i * tq) // tk

    def tile(j, masked):
        ks = pl.ds(pl.multiple_of(j * tk, tk), tk)

        def logits(h):
            s = lax.dot_general(q[:, lanes[h]], kb_ref[ks, lanes[h]], NT_DIMS, preferred_element_type=F32)
            s = s + cq[h] - _select_row(ct_ref[0, ck_grp[h], ks], ck_sel[h]) * LOG2_E
            if masked:
                r = lax.broadcasted_iota(jnp.int32, (tq, tk), 0) + (i * tq - j * tk)
                cc = lax.broadcasted_iota(jnp.int32, (tq, tk), 1)
                s = jnp.where(cc <= r, s, -jnp.inf)
            return s

        def finish(h, s):
            m_old = m_ref[:, lanes[h]]
            m_new = jnp.maximum(m_old, jnp.max(s, axis=-1, keepdims=True))
            p = jnp.exp2(s - _lane_tiles(m_new, tk // LANES))
            alpha = jnp.exp2(m_old - m_new)
            pv = jnp.dot(p.astype(BF16), vb_ref[ks, lanes[h]], preferred_element_type=F32)
            l_ref[:, lanes[h]] = alpha * l_ref[:, lanes[h]] + jnp.sum(p, axis=-1, keepdims=True)
            acc_ref[:, lanes[h]] = alpha * acc_ref[:, lanes[h]] + pv
            m_ref[:, lanes[h]] = m_new

        s_prev = logits(0)
        for h in range(1, n_heads):
            s_cur = logits(h)
            finish(h - 1, s_prev)
            s_prev = s_cur
        finish(n_heads - 1, s_prev)

    def body(j, carry):
        tile(j, False)
        return carry

    lax.fori_loop(0, n_full, body, 0)
    tile(n_full, True)
    o_ref[0] = (acc_ref[...] / l_ref[...]).astype(o_ref.dtype)


def fox_attention_prompt(q, k, v, c, ct, cast_jobs=()):
    bsz, t, w = q.shape
    d = HEAD_DIM
    heads = w // d
    hb = 4 if heads % 4 == 0 else 1
    tq = _tile(t, 256)
    tk = _tile(t, 256)
    assert tk % tq == 0
    blk = pl.BlockSpec((1, tq, hb * d), lambda b, h, i: (b, i, h))
    full = pl.BlockSpec((1, t, hb * d), lambda b, h, i: (b, 0, h))
    grid = (bsz, heads // hb, t // tq)
    side = _side_cast_plan(cast_jobs, grid[0] * grid[1] * grid[2],
                           lambda b, h, i: (b * grid[1] + h) * grid[2] + i)
    outs = pl.pallas_call(
        functools.partial(_fox_prompt_body, tq=tq, tk=tk, n_heads=hb, n_side=len(cast_jobs)),
        out_shape=[jax.ShapeDtypeStruct((bsz, t, w), BF16)] + side[2],
        grid=grid,
        in_specs=[blk, full, full,
                  pl.BlockSpec((1, tq, LANES), lambda b, h, i: (b, i, 0)),
                  pl.BlockSpec((1, LANES, t), lambda b, h, i: (b, 0, 0))] + side[0],
        out_specs=[blk] + side[1],
        scratch_shapes=[pltpu.VMEM((t, hb * d), BF16), pltpu.VMEM((t, hb * d), BF16),
                        pltpu.VMEM((tq, hb * d), F32), pltpu.VMEM((tq, hb * d), F32),
                        pltpu.VMEM((tq, hb * d), F32)],
        compiler_params=_cparams(("arbitrary", "arbitrary", "arbitrary")),
        name="fox_attention_prompt",
    )(q, k, v, c, ct, *side[3])
    return outs[0], list(outs[1:])


def _suffix_body(pt_ref, *refs, pages_per_step):
    del pt_ref
    x_refs = refs[:pages_per_step]
    d_ref, carry_ref = refs[pages_per_step:]

    @pl.when(pl.program_id(1) == 0)
    def _():
        carry_ref[...] = jnp.zeros_like(carry_ref)

    n = x_refs[0].shape[2]
    later = (lax.broadcasted_iota(jnp.int32, (n, n), 0) > lax.broadcasted_iota(jnp.int32, (n, n), 1)).astype(F32)
    within = [jnp.dot(x_ref[0], later, precision=HIGHEST, preferred_element_type=F32) for x_ref in x_refs]
    carry = carry_ref[...]
    for r in range(pages_per_step):
        d_ref[0, pages_per_step - 1 - r] = within[r] + carry
        carry = carry + jnp.sum(x_refs[r][0], axis=1, keepdims=True)
    carry_ref[...] = carry


def suffix_log_forget(logf_t, page_table):
    bsz, n_pages = page_table.shape
    _, heads, page = logf_t.shape
    r_pages = 16 if n_pages % 16 == 0 else 1
    steps = n_pages // r_pages

    def page_spec(r):
        return pl.BlockSpec((1, heads, page), lambda b, p, pt: (pt[b, n_pages - 1 - (p * r_pages + r)], 0, 0))

    grid_spec = pltpu.PrefetchScalarGridSpec(
        num_scalar_prefetch=1,
        grid=(bsz, steps),
        in_specs=[page_spec(r) for r in range(r_pages)],
        out_specs=pl.BlockSpec((1, r_pages, heads, page), lambda b, p, pt: (b, steps - 1 - p, 0, 0)),
        scratch_shapes=[pltpu.VMEM((heads, page), F32)],
    )
    return pl.pallas_call(
        functools.partial(_suffix_body, pages_per_step=r_pages),
        out_shape=jax.ShapeDtypeStruct((bsz, n_pages, heads, page), F32),
        grid_spec=grid_spec,
        compiler_params=_cparams(("parallel", "arbitrary")),
        name="suffix_log_forget",
    )(page_table, *([logf_t] * r_pages))


def _fox_decode_body(pt_ref, q_ref, *rest, n_heads, pages_per_step, page):
    del pt_ref
    r_pages = pages_per_step
    k_refs = rest[:r_pages]
    v_refs = rest[r_pages:2 * r_pages]
    (d_ref, cn_ref, cnt_ref, kn_ref, vn_ref, o_ref, m_ref, l_ref, acc_ref) = rest[2 * r_pages:]
    p_idx = pl.program_id(1)
    d = HEAD_DIM
    tq = q_ref.shape[1]
    scale = d ** -0.5
    heads = range(n_heads)
    lanes = [slice(h * d, (h + 1) * d) for h in heads]

    @pl.when(p_idx == 0)
    def _():
        m_ref[...] = jnp.full_like(m_ref, -jnp.inf)
        l_ref[...] = jnp.zeros_like(l_ref)
        acc_ref[...] = jnp.zeros_like(acc_ref)

    def update(s, v_tiles):
        m_old = [m_ref[:, lanes[h]] for h in heads]
        m_new = [jnp.maximum(m_old[h], jnp.max(s[h], axis=-1, keepdims=True)) for h in heads]
        p = [jnp.exp(s[h] - _lane_tiles(m_new[h], s[h].shape[1] // LANES)) for h in heads]
        alpha = [jnp.exp(m_old[h] - m_new[h]) for h in heads]
        pb = [x.astype(BF16) for x in p]
        pv = []
        for h in heads:
            acc = jnp.dot(pb[h][:, 0:d], v_tiles[h][0], preferred_element_type=F32)
            for r in range(1, len(v_tiles[h])):
                acc = acc + jnp.dot(pb[h][:, r * d:(r + 1) * d], v_tiles[h][r], preferred_element_type=F32)
            pv.append(acc)
        for h in heads:
            l_ref[:, lanes[h]] = alpha[h] * l_ref[:, lanes[h]] + jnp.sum(p[h], axis=-1, keepdims=True)
            acc_ref[:, lanes[h]] = alpha[h] * acc_ref[:, lanes[h]] + pv[h]
            m_ref[:, lanes[h]] = m_new[h]

    qb = q_ref[0].astype(BF16)

    def head_rows(ref, h):
        return ref[0, pl.ds(h, page, stride=n_heads), :].astype(BF16)

    s = []
    for h in heads:
        parts = [lax.dot_general(qb[:, lanes[h]], head_rows(k_refs[r], h), NT_DIMS, preferred_element_type=F32) * scale
                 + (cn_ref[0, :, h:h + 1] + d_ref[0, r, h:h + 1, :]) for r in range(r_pages)]
        s.append(parts[0] if r_pages == 1 else jnp.concatenate(parts, axis=1))
    update(s, [[head_rows(v_refs[r], h) for r in range(r_pages)] for h in heads])

    @pl.when(p_idx == pl.num_programs(1) - 1)
    def _():
        n_new = kn_ref.shape[1]
        r_i = lax.broadcasted_iota(jnp.int32, (tq, n_new), 0)
        c_i = lax.broadcasted_iota(jnp.int32, (tq, n_new), 1)
        s_new = [lax.dot_general(qb[:, lanes[h]], kn_ref[0, :, lanes[h]].astype(BF16), NT_DIMS,
                                 preferred_element_type=F32) * scale
                 + (cn_ref[0, :, h:h + 1] - cnt_ref[0, h:h + 1, :]) for h in heads]
        s_new = [jnp.where(c_i <= r_i, x, -jnp.inf) for x in s_new]
        update(s_new, [[vn_ref[0, :, lanes[h]].astype(BF16)] for h in heads])
        o_ref[0] = (acc_ref[...] / l_ref[...]).astype(o_ref.dtype)


def fox_attention_decode(q, cache_k, cache_v, d_suffix, cn, cnt, k_new, v_new, page_table):
    bsz, tq, w = q.shape
    heads = w // HEAD_DIM
    rows = cache_k.shape[1]
    page = rows // heads
    n_pages = page_table.shape[1]
    tn = k_new.shape[1]
    r_pages = 4 if n_pages % 4 == 0 else 1
    steps = n_pages // r_pages

    def page_spec(r):
        return pl.BlockSpec((1, rows, HEAD_DIM), lambda b, p, pt: (pt[b, p * r_pages + r], 0, 0))

    whole = lambda shape: pl.BlockSpec((1,) + shape, lambda b, p, pt: (b, 0, 0))
    grid_spec = pltpu.PrefetchScalarGridSpec(
        num_scalar_prefetch=1,
        grid=(bsz, steps),
        in_specs=[whole((tq, w))] + [page_spec(r) for r in range(r_pages)] * 2
                 + [pl.BlockSpec((1, r_pages, heads, page), lambda b, p, pt: (b, p, 0, 0)),
                    pl.BlockSpec((1, tq, LANES), lambda b, p, pt: (b, 0, 0)),
                    whole((LANES, tn)), whole((tn, w)), whole((tn, w))],
        out_specs=pl.BlockSpec((1, tq, w), lambda b, p, pt: (b, 0, 0)),
        scratch_shapes=[pltpu.VMEM((tq, w), F32), pltpu.VMEM((tq, w), F32), pltpu.VMEM((tq, w), F32)],
    )
    return pl.pallas_call(
        functools.partial(_fox_decode_body, n_heads=heads, pages_per_step=r_pages, page=page),
        out_shape=jax.ShapeDtypeStruct((bsz, tq, w), BF16),
        grid_spec=grid_spec,
        compiler_params=_cparams(("parallel", "arbitrary")),
        name="fox_attention_decode",
    )(page_table, q, *([cache_k] * r_pages), *([cache_v] * r_pages), d_suffix, cn, cnt, k_new, v_new)


def _pad_rows(x, rows):
    pad = [(0, 0)] * x.ndim
    pad[1] = (0, rows - x.shape[1])
    return jnp.pad(x, pad)


def _trunk(hp, hs, dims_p, dims_s, mem_p, mem_s, hg_s0, decode, wts):
    d_model = hp.shape[1]
    tok_w = wts["tok_w"]
    mem_w = d_model - tok_w
    mem_heads = mem_p[0].shape[3]
    to4 = lambda a: a.reshape(a.shape[:3] + (mem_w,))
    mem_p = (to4(mem_p[0]), to4(mem_p[1]))
    mem_s = (to4(mem_s[0]), to4(mem_s[1]))
    n_a = wts["w_in_a"].shape[0]
    depth = wts["w_out"].shape[0]
    (bp, tp), (bs, ts) = dims_p, dims_s
    states_p, states_s = [], []
    hn_p = rmsnorm_bf16(hp, wts["norm_pre_mix"][0])
    hn_s = rmsnorm_bf16(hs, wts["norm_pre_mix"][0])
    kv_p = kv_s = None
    hkv_p = hkv_s = None
    for layer in range(depth):
        cast_jobs = [(wts["w_out"], layer), (wts["w_mlp_up"], layer), (wts["w_mlp_down"], layer)]
        if layer == n_a:
            wkv = wts["w_kv"]
            k_p, k_s = matmul(hkv_p, hkv_s, wkv, 0, tok_w, trans_w=True)
            v_p, v_s = matmul(hkv_p, hkv_s, wkv, tok_w, tok_w, trans_w=True)
            lf_p, lf_s = matmul(hkv_p, hkv_s, wts["w_fg"], 0, LANES, kind="logsig", vecs=(wts["b_fg"],),
                                trans_w=True)
            kv_p, kv_s = (k_p, v_p, lf_p), (k_s, v_s, lf_s)
        if layer < n_a:
            w_in = wts["w_in_a"]
            q = matmul(hn_p, hn_s, w_in, 0, tok_w, kind="silu", layer=layer)
            f = matmul(hn_p, hn_s, w_in, tok_w, tok_w, kind="fgate", vecs=(wts["lb"][layer],), layer=layer)
            v = matmul(hn_p, hn_s, w_in, 2 * tok_w, tok_w, layer=layer)
            g = matmul(hn_p, hn_s, w_in, 3 * tok_w, tok_w, kind="silu", layer=layer)
            zm_p, zm_s = matmul(hn_p, hn_s, w_in, 4 * tok_w, mem_w, layer=layer)
            gain = wts["hg_out_norm"][layer]
            shp_p, shp_s = (bp, tp, tok_w), (bs, ts, tok_w)
            o_p, st_p, w_bf = hgrn2(q[0].reshape(shp_p), f[0].reshape(shp_p), v[0].reshape(shp_p),
                                    g[0].reshape(shp_p), gain, None, cast_jobs=cast_jobs)
            o_s, st_s, _ = hgrn2(q[1].reshape(shp_s), f[1].reshape(shp_s), v[1].reshape(shp_s), g[1].reshape(shp_s),
                                 gain, hg_s0[layer], wts["t_valid"])
            states_p.append(st_p)
            states_s.append(st_s)
        else:
            w_in = wts["w_in_b"]
            q_p, q_s = matmul(hn_p, hn_s, w_in, 0, tok_w, layer=layer - n_a)
            zm_p, zm_s = matmul(hn_p, hn_s, w_in, tok_w, mem_w, layer=layer - n_a)
            c, ct = cumsum_tokens(kv_p[2].reshape(bp, tp, LANES))
            o_p, w_bf = fox_attention_prompt(q_p.reshape(bp, tp, tok_w), kv_p[0].reshape(bp, tp, tok_w),
                                             kv_p[1].reshape(bp, tp, tok_w), c, ct, cast_jobs=cast_jobs)
            cn, cnt = cumsum_tokens(_pad_rows(kv_s[2].reshape(bs, ts, LANES), LANES))
            kn = _pad_rows(kv_s[0].reshape(bs, ts, tok_w), LANES)
            vn = _pad_rows(kv_s[1].reshape(bs, ts, tok_w), LANES)
            o_s = fox_attention_decode(q_s.reshape(bs, ts, tok_w), decode["cache_k"], decode["cache_v"],
                                       decode["d_suffix"], cn, cnt, kn, vn, decode["page_table"])
        om_p = memory_attention(zm_p.reshape(bp, tp, mem_w), mem_p[0], mem_p[1], layer, mem_heads)
        om_s = memory_attention(zm_s.reshape(bs, ts, mem_w), mem_s[0], mem_s[1], layer, mem_heads)
        w_out_b, w_up_b, w_down_b = [w[None] for w in w_bf]
        mixed_p, mixed_s = matmul((o_p.reshape(bp * tp, tok_w), om_p.reshape(bp * tp, mem_w)),
                                  (o_s.reshape(bs * ts, tok_w), om_s.reshape(bs * ts, mem_w)),
                                  w_out_b, 0, d_model)
        hp, (hn_p,) = post_norm_residual(mixed_p, hp, wts["norm_post_mix"][layer], [wts["norm_pre_mlp"][layer]])
        hs, (hn_s,) = post_norm_residual(mixed_s, hs, wts["norm_post_mix"][layer], [wts["norm_pre_mlp"][layer]])
        u_p, u_s = matmul(hn_p, hn_s, w_up_b, 0, w_up_b.shape[2], kind="relu2", out_dtype=BF16)
        down_p, down_s = matmul_long_k(u_p, u_s, w_down_b, 0)
        nxt = []
        if layer + 1 < depth:
            nxt = [wts["norm_pre_mix"][layer + 1]] + ([wts["kv_norm"]] if layer + 1 == n_a else [])
        hp, hns_p = post_norm_residual(down_p, hp, wts["norm_post_mlp"][layer], nxt)
        hs, hns_s = post_norm_residual(down_s, hs, wts["norm_post_mlp"][layer], nxt)
        if nxt:
            hn_p, hn_s = hns_p[0], hns_s[0]
            if layer + 1 == n_a:
                hkv_p, hkv_s = hns_p[1], hns_s[1]
    return (hp, jnp.stack(states_p), kv_p), (hs, jnp.stack(states_s), kv_s)


def kernel(x_prompt, x_sample, state_hgrn, cache_k, cache_v, cache_logf, cache_mem_k, cache_mem_v, page_table, mem_prompt, norm_pre_mix, norm_post_mix, norm_pre_mlp, norm_post_mlp, w_in_a, hg_lb_logits, hg_out_norm, w_in_b, kv_norm, w_kv_shared, b_fgate, w_out, mem_norm, w_mem_kv, w_mlp_up, w_mlp_down):
    bp, tp, d_model = x_prompt.shape
    bs, ts, _ = x_sample.shape
    fox_heads, fox_dim = cache_k.shape[2], cache_k.shape[3]
    assert fox_dim == HEAD_DIM and state_hgrn.shape[-1] == HEAD_DIM
    tok_w = fox_heads * fox_dim
    mem_w = d_model - tok_w
    mem_len = mem_prompt.shape[1]
    depth = w_out.shape[0]
    n_pool, page = cache_k.shape[0], cache_k.shape[1]
    t_pad = max(SUBLANES, ts)

    lb_all = jnp.cumsum(jax.nn.softmax(hg_lb_logits.astype(F32), axis=0), axis=0)
    wts = {
        "tok_w": tok_w, "t_valid": ts,
        "norm_pre_mix": norm_pre_mix, "norm_post_mix": norm_post_mix,
        "norm_pre_mlp": norm_pre_mlp, "norm_post_mlp": norm_post_mlp,
        "kv_norm": kv_norm, "hg_out_norm": hg_out_norm, "lb": lb_all,
        "w_in_a": w_in_a, "w_in_b": w_in_b,
        "w_kv": w_kv_shared.T[None],
        "w_fg": jnp.pad(w_kv_shared.T[2 * tok_w:], ((0, LANES - fox_heads), (0, 0)))[None],
        "b_fg": jnp.pad(b_fgate.astype(F32), (0, LANES - fox_heads)),
        "w_out": w_out, "w_mlp_up": w_mlp_up, "w_mlp_down": w_mlp_down,
    }

    mem_n = rmsnorm_bf16(mem_prompt.reshape(bp * mem_len, d_model), mem_norm)
    mem_k_flat = matmul(mem_n, None, w_mem_kv, 0, mem_w, all_layers=True)
    mem_v_flat = matmul(mem_n, None, w_mem_kv, mem_w, mem_w, all_layers=True)
    mem_heads = cache_mem_k.shape[3]
    mem_shape = (depth, bp, mem_len, mem_heads, mem_w // mem_heads)
    mem_k_prompt = mem_k_flat.reshape(mem_shape)
    mem_v_prompt = mem_v_flat.reshape(mem_shape)

    decode = {
        "cache_k": cache_k.reshape(n_pool, page * fox_heads, fox_dim),
        "cache_v": cache_v.reshape(n_pool, page * fox_heads, fox_dim),
        "d_suffix": suffix_log_forget(cache_logf.transpose(0, 2, 1), page_table),
        "page_table": page_table,
    }
    xs = _pad_rows(x_sample, t_pad).reshape(bs * t_pad, d_model)
    (y_p, hg_p, (k_p, v_p, lf_p)), (y_s, hg_s, (k_s, v_s, lf_s)) = _trunk(
        x_prompt.reshape(bp * tp, d_model), xs, (bp, tp), (bs, t_pad),
        (mem_k_prompt, mem_v_prompt), (cache_mem_k, cache_mem_v), state_hgrn, decode, wts)

    def unpad(a, width):
        return a.reshape(bs, t_pad, width)[:, :ts]

    return (y_p.reshape(bp, tp, d_model),
            unpad(y_s, d_model),
            hg_p.astype(state_hgrn.dtype),
            hg_s.astype(state_hgrn.dtype),
            k_p.reshape(bp, tp, fox_heads, fox_dim),
            v_p.reshape(bp, tp, fox_heads, fox_dim),
            lf_p[:, :fox_heads].reshape(bp, tp, fox_heads),
            unpad(k_s, tok_w).reshape(bs, ts, fox_heads, fox_dim),
            unpad(v_s, tok_w).reshape(bs, ts, fox_heads, fox_dim),
            unpad(lf_s, LANES)[:, :, :fox_heads],
            mem_k_prompt, mem_v_prompt)
```

```python
import functools

import jax
import jax.numpy as jnp
from jax import lax
from jax.experimental import pallas as pl
from jax.experimental.pallas import tpu as pltpu

F32 = jnp.float32
BF16 = jnp.bfloat16

NORM_EPS = 1e-6
HEAD_DIM = 128
LANES = 128
SUBLANES = 8
VMEM_LIMIT = 52 * 1024 * 1024
HG_CHUNK = 64
HG_SAFE_DECAY = 60.0
HIGHEST = lax.Precision.HIGHEST
LOG2_E = 1.4426950408889634

NT_DIMS = (((1,), (1,)), ((), ()))
TN_DIMS = (((0,), (0,)), ((), ()))


def _cparams(semantics):
    return pltpu.CompilerParams(dimension_semantics=semantics, vmem_limit_bytes=VMEM_LIMIT)


def _tile(n, want):
    if n <= want:
        return n
    t = want
    while t >= 16:
        if n % t == 0:
            return t
        t -= 16
    return n


def _lane_tiles(x, n):
    return x if n == 1 else jnp.concatenate([x] * n, axis=1)


def _cumsum_rows(tri, x):
    w = x.shape[1]
    hi = x.astype(BF16)
    rest = x - hi.astype(F32)
    mid = rest.astype(BF16)
    lo = (rest - mid.astype(F32)).astype(BF16)
    y = jnp.dot(tri, jnp.concatenate([hi, mid, lo], axis=1), preferred_element_type=F32)
    return y[:, :w] + y[:, w:2 * w] + y[:, 2 * w:]


def _select_row(group, sel):
    return jnp.sum(jnp.where(sel, group, 0.0), axis=0, keepdims=True)


def _rms_scale(x):
    return lax.rsqrt(jnp.mean(x * x, axis=-1, keepdims=True) + NORM_EPS)


def _rmsnorm_body(x_ref, g_ref, o_ref):
    x = x_ref[...]
    o_ref[...] = (x * _rms_scale(x) * g_ref[...]).astype(o_ref.dtype)


def rmsnorm_bf16(x, gain):
    m, d = x.shape
    tm = _tile(m, 256)
    return pl.pallas_call(
        _rmsnorm_body,
        out_shape=jax.ShapeDtypeStruct((m, d), BF16),
        grid=(m // tm,),
        in_specs=[pl.BlockSpec((tm, d), lambda i: (i, 0)), pl.BlockSpec((1, d), lambda i: (0, 0))],
        out_specs=pl.BlockSpec((tm, d), lambda i: (i, 0)),
        compiler_params=_cparams(("parallel",)),
        name="rmsnorm",
    )(x, gain.reshape(1, d))


def _post_body(m_ref, h_ref, gp_ref, *rest, n_next):
    g_next = rest[:n_next]
    ho_ref = rest[n_next]
    hn_refs = rest[n_next + 1:]
    m = m_ref[...]
    h = h_ref[...] + m * _rms_scale(m) * gp_ref[...]
    ho_ref[...] = h
    if n_next:
        hs = h * _rms_scale(h)
        for g_ref, hn_ref in zip(g_next, hn_refs):
            hn_ref[...] = (hs * g_ref[...]).astype(hn_ref.dtype)


def post_norm_residual(mixed, h, gain_post, next_gains):
    m, d = h.shape
    tm = _tile(m, 256)
    n_next = len(next_gains)
    row = pl.BlockSpec((tm, d), lambda i: (i, 0))
    vec = pl.BlockSpec((1, d), lambda i: (0, 0))
    outs = pl.pallas_call(
        functools.partial(_post_body, n_next=n_next),
        out_shape=[jax.ShapeDtypeStruct((m, d), F32)] + [jax.ShapeDtypeStruct((m, d), BF16)] * n_next,
        grid=(m // tm,),
        in_specs=[row, row, vec] + [vec] * n_next,
        out_specs=[row] * (1 + n_next),
        compiler_params=_cparams(("parallel",)),
        name="post_norm_residual",
    )(mixed, h, gain_post.reshape(1, d), *[g.reshape(1, d) for g in next_gains])
    return outs[0], list(outs[1:])


def _epilogue(kind, acc, vecs):
    if kind == "none":
        return acc
    if kind == "silu":
        return acc * jax.nn.sigmoid(acc)
    if kind == "fgate":
        lb = vecs[0][...]
        return lb + (1.0 - lb) * jax.nn.sigmoid(acc)
    if kind == "relu2":
        r = jnp.maximum(acc, 0.0)
        return r * r
    if kind == "logsig":
        return jax.nn.log_sigmoid(acc + vecs[0][...])
    raise ValueError(kind)


def _mm_body(*refs, kind, n_vec, n_seg, has_second, cast_w, trans_w):
    x_refs, refs = refs[:n_seg], refs[n_seg:]
    x2_refs = ()
    if has_second:
        x2_refs, refs = refs[:n_seg], refs[n_seg:]
    w_ref, rest = refs[0], refs[1:]
    vecs = rest[:n_vec]
    outs = rest[n_vec:]
    o_ref = outs[0]
    i = pl.program_id(2)
    if cast_w:
        wb_ref = outs[-1]

        @pl.when(i == 0)
        def _():
            wb_ref[...] = w_ref[...].astype(BF16)
    else:
        wb_ref = w_ref

    def mm(a_refs):
        acc, off = None, 0
        for a_ref in a_refs:
            ks = a_ref.shape[1]
            if trans_w:
                part = lax.dot_general(a_ref[...], wb_ref[:, off:off + ks], NT_DIMS, preferred_element_type=F32)
            else:
                part = jnp.dot(a_ref[...], wb_ref[off:off + ks, :], preferred_element_type=F32)
            acc = part if acc is None else acc + part
            off += ks
        return acc

    o_ref[...] = _epilogue(kind, mm(x_refs), vecs).astype(o_ref.dtype)

    if has_second:
        o2_ref = outs[1]

        @pl.when(i == pl.num_programs(2) - 1)
        def _():
            o2_ref[...] = _epilogue(kind, mm(x2_refs), vecs).astype(o2_ref.dtype)


def matmul(x, x2, w, col_off, n_cols, kind="none", vecs=(), out_dtype=F32, layer=0, all_layers=False,
           trans_w=False):
    xs = x if isinstance(x, tuple) else (x,)
    m = xs[0].shape[0]
    k = sum(a.shape[1] for a in xs)
    nl = w.shape[0] if all_layers else 1
    cast_w = w.dtype != BF16
    tm = _tile(m, 1024)
    tn = _tile(n_cols, 512 if cast_w else 1024)
    assert col_off % tn == 0 and n_cols % tn == 0
    jo = col_off // tn
    has_second = x2 is not None
    vec_arrs = [v.reshape(1, n_cols).astype(F32) for v in vecs]
    in_specs = [pl.BlockSpec((tm, a.shape[1]), lambda l, j, i: (i, 0)) for a in xs]
    args = list(xs)
    out_shape = [jax.ShapeDtypeStruct((nl, m, n_cols), out_dtype)]
    out_specs = [pl.BlockSpec((None, tm, tn), lambda l, j, i: (l, i, j))]
    if has_second:
        x2s = x2 if isinstance(x2, tuple) else (x2,)
        assert [a.shape[1] for a in x2s] == [a.shape[1] for a in xs]
        m2 = x2s[0].shape[0]
        in_specs += [pl.BlockSpec((m2, a.shape[1]), lambda l, j, i: (0, 0)) for a in x2s]
        args += list(x2s)
        out_shape.append(jax.ShapeDtypeStruct((nl, m2, n_cols), out_dtype))
        out_specs.append(pl.BlockSpec((None, m2, tn), lambda l, j, i: (l, 0, j)))
    w_layer = lambda l: l if all_layers else layer
    if trans_w:
        w_block = (tn, k)
        in_specs.append(pl.BlockSpec((None, tn, k), lambda l, j, i: (w_layer(l), jo + j, 0)))
    else:
        w_block = (k, tn)
        in_specs.append(pl.BlockSpec((None, k, tn), lambda l, j, i: (w_layer(l), 0, jo + j)))
    in_specs += [pl.BlockSpec((1, tn), lambda l, j, i: (0, j))] * len(vec_arrs)
    outs = pl.pallas_call(
        functools.partial(_mm_body, kind=kind, n_vec=len(vec_arrs), n_seg=len(xs), has_second=has_second,
                          cast_w=cast_w, trans_w=trans_w),
        out_shape=out_shape,
        grid=(nl, n_cols // tn, m // tm),
        in_specs=in_specs,
        out_specs=out_specs,
        scratch_shapes=[pltpu.VMEM(w_block, BF16)] if cast_w else [],
        compiler_params=_cparams(("parallel", "parallel", "arbitrary")),
        name="matmul_" + kind,
    )(*args, w, *vec_arrs)
    if all_layers:
        return outs[0]
    return (outs[0][0], outs[1][0]) if has_second else outs[0][0]


def _mm_acc_body(x_ref, x2_ref, w_ref, o_ref, o2_ref):
    i = pl.program_id(1)
    kk = pl.program_id(2)

    @pl.when(kk == 0)
    def _():
        o_ref[...] = jnp.zeros_like(o_ref)

    o_ref[...] += jnp.dot(x_ref[...], w_ref[...], preferred_element_type=F32)

    first = i == 0

    @pl.when(jnp.logical_and(first, kk == 0))
    def _():
        o2_ref[...] = jnp.zeros_like(o2_ref)

    @pl.when(first)
    def _():
        o2_ref[...] += jnp.dot(x2_ref[...], w_ref[...], preferred_element_type=F32)


def matmul_long_k(x, x2, w, layer):
    m, k = x.shape
    m2 = x2.shape[0]
    n = w.shape[2]
    tm = _tile(m, 1024)
    tn = _tile(n, 1024)
    tk = _tile(k, 4096)
    return pl.pallas_call(
        _mm_acc_body,
        out_shape=[jax.ShapeDtypeStruct((m, n), F32), jax.ShapeDtypeStruct((m2, n), F32)],
        grid=(n // tn, m // tm, k // tk),
        in_specs=[pl.BlockSpec((tm, tk), lambda j, i, kk: (i, kk)),
                  pl.BlockSpec((m2, tk), lambda j, i, kk: (0, kk)),
                  pl.BlockSpec((None, tk, tn), lambda j, i, kk: (layer, kk, j))],
        out_specs=[pl.BlockSpec((tm, tn), lambda j, i, kk: (i, j)),
                   pl.BlockSpec((m2, tn), lambda j, i, kk: (0, j))],
        compiler_params=_cparams(("parallel", "arbitrary", "arbitrary")),
        name="matmul_long_k",
    )(x, x2, w)


def _side_cast_plan(jobs, n_steps, step_of):
    in_specs, out_specs, out_shapes, arrays = [], [], [], []
    for w, layer in jobs:
        _, k, c = w.shape
        rows = next(r for r in range(16, k + 1, 16) if k % r == 0 and k // r <= n_steps)
        last = k // rows - 1
        in_specs.append(pl.BlockSpec(
            (None, rows, c), lambda *g, layer=layer, last=last: (layer, jnp.minimum(step_of(*g), last), 0)))
        out_specs.append(pl.BlockSpec((rows, c), lambda *g, last=last: (jnp.minimum(step_of(*g), last), 0)))
        out_shapes.append(jax.ShapeDtypeStruct((k, c), BF16))
        arrays.append(w)
    return in_specs, out_specs, out_shapes, arrays


def _side_cast(src_refs, dst_refs):
    for src, dst in zip(src_refs, dst_refs):
        dst[...] = src[...].astype(BF16)


def _hgrn_body(*refs, n_heads, t_chunk, t_valid, has_s0, n_side):
    q_ref, f_ref, v_ref, g_ref, gain_ref = refs[:5]
    refs = refs[5:]
    s0_ref = None
    if has_s0:
        s0_ref, refs = refs[0], refs[1:]
    side_in, refs = refs[:n_side], refs[n_side:]
    o_ref, sout_ref = refs[:2]
    side_out = refs[2:2 + n_side]
    st_ref, o_scr, lg_scr = refs[2 + n_side:]
    _side_cast(side_in, side_out)
    t_idx = pl.program_id(2)
    d = HEAD_DIM
    c = min(HG_CHUNK, t_chunk)
    n_sub = t_chunk // c
    heads = range(n_heads)
    lanes = [slice(h * d, (h + 1) * d) for h in heads]

    @pl.when(t_idx == 0)
    def _():
        for h in heads:
            if has_s0:
                st_ref[h] = s0_ref[0, h].T
            else:
                st_ref[h] = jnp.zeros((d, d), F32)

    if t_valid is not None:
        assert n_sub == 1 and t_valid <= t_chunk
        o_scr[...] = jnp.zeros_like(o_scr)
        sts = [st_ref[h] for h in heads]
        for t in range(t_valid):
            f_t = f_ref[0, t:t + 1, :]
            v_t = v_ref[0, t:t + 1, :]
            q_t = q_ref[0, t:t + 1, :]
            v_cols = [jnp.broadcast_to(v_t[:, lanes[h]], (d, d)).T for h in heads]
            sts = [sts[h] * f_t[:, lanes[h]] + v_cols[h] * (1.0 - f_t[:, lanes[h]]) for h in heads]
            q8s = [jnp.broadcast_to(q_t[:, lanes[h]], (SUBLANES, d)).astype(BF16) for h in heads]
            o8s = [lax.dot_general(q8s[h], sts[h].astype(BF16), NT_DIMS, preferred_element_type=F32) for h in heads]
            for h in heads:
                o_scr[t:t + 1, lanes[h]] = o8s[h][0:1, :]
        for h in heads:
            st_ref[h] = sts[h]

    def token_steps(h, first, count):
        hs = pl.ds(h * d, d)

        def step(t, st):
            base = pl.multiple_of(((first + t) // SUBLANES) * SUBLANES, SUBLANES)
            grp = pl.ds(base, SUBLANES)
            sel = lax.broadcasted_iota(jnp.int32, (SUBLANES, d), 0) == (first + t - base)
            f_t = _select_row(f_ref[0, grp, hs], sel)
            v_col = jnp.broadcast_to(_select_row(v_ref[0, grp, hs], sel), (d, d)).T
            st = st * f_t + v_col * (1.0 - f_t)
            q8 = jnp.broadcast_to(_select_row(q_ref[0, grp, hs], sel), (SUBLANES, d)).astype(BF16)
            o8 = lax.dot_general(q8, st.astype(BF16), NT_DIMS, preferred_element_type=F32)
            o_scr[grp, hs] = jnp.where(sel, o8, o_scr[grp, hs])
            return st

        st_ref[h] = lax.fori_loop(0, count, step, st_ref[h])

    row = lax.broadcasted_iota(jnp.int32, (c, c), 0)
    col = lax.broadcasted_iota(jnp.int32, (c, c), 1)
    causal = row >= col
    tri = causal.astype(BF16)

    def chunk_steps(sc):
        rows = pl.ds(sc * c, c)
        b = _cumsum_rows(tri, lg_scr[rows, :])
        e_last = jnp.exp(b[c - 1:c, :])
        qt = (q_ref[0, rows, :] * jnp.exp(b)).astype(BF16)
        kt = (1.0 - f_ref[0, rows, :]) * jnp.exp(-b)
        kd = (kt * e_last).astype(BF16)
        kt = kt.astype(BF16)
        vb = v_ref[0, rows, :].astype(BF16)
        stb = [st_ref[h].astype(BF16) for h in heads]
        att = [lax.dot_general(qt[:, lanes[h]], kt[:, lanes[h]], NT_DIMS, preferred_element_type=F32) for h in heads]
        o_inter = [lax.dot_general(qt[:, lanes[h]], stb[h], NT_DIMS, preferred_element_type=F32) for h in heads]
        upd = [lax.dot_general(vb[:, lanes[h]], kd[:, lanes[h]], TN_DIMS, preferred_element_type=F32) for h in heads]
        att = [jnp.where(causal, a, 0.0).astype(BF16) for a in att]
        o_intra = [jnp.dot(att[h], vb[:, lanes[h]], preferred_element_type=F32) for h in heads]
        for h in heads:
            o_scr[rows, lanes[h]] = o_intra[h] + o_inter[h]
            st_ref[h] = st_ref[h] * e_last[:, lanes[h]] + upd[h]

    if t_valid is None:
        lg_scr[...] = jnp.log(f_ref[0])
        totals = [jnp.min(jnp.sum(lg_scr[pl.ds(sc * c, c), :], axis=0, keepdims=True)) for sc in range(n_sub)]
        safe = functools.reduce(jnp.minimum, totals) > -HG_SAFE_DECAY

        @pl.when(safe)
        def _():
            for sc in range(n_sub):
                chunk_steps(sc)

        @pl.when(jnp.logical_not(safe))
        def _():
            for h in heads:
                token_steps(h, 0, t_chunk)

    for h in heads:
        o = o_scr[:, lanes[h]]
        o = o * _rms_scale(o) * gain_ref[:, lanes[h]]
        o_ref[0, :, lanes[h]] = (o * g_ref[0, :, lanes[h]]).astype(o_ref.dtype)

    @pl.when(t_idx == pl.num_programs(2) - 1)
    def _():
        for h in heads:
            sout_ref[0, h] = st_ref[h].T


def hgrn2(q, f, v, g, gain, s0, t_valid=None, cast_jobs=()):
    bsz, t, w = q.shape
    d = HEAD_DIM
    heads = w // d
    hb = 6 if heads % 6 == 0 else (4 if heads % 4 == 0 else 1)
    tc = _tile(t, 256)
    has_s0 = s0 is not None
    tok = pl.BlockSpec((1, tc, hb * d), lambda b, hg, ti: (b, ti, hg))
    in_specs = [tok, tok, tok, tok, pl.BlockSpec((1, hb * d), lambda b, hg, ti: (0, hg))]
    args = [q, f, v, g, gain.reshape(1, w)]
    state_spec = pl.BlockSpec((1, hb, d, d), lambda b, hg, ti: (b, hg, 0, 0))
    if has_s0:
        in_specs.append(state_spec)
        args.append(s0)
    grid = (bsz, heads // hb, t // tc)
    side = _side_cast_plan(cast_jobs, grid[0] * grid[1] * grid[2],
                           lambda b, hg, ti: (b * grid[1] + hg) * grid[2] + ti)
    outs = pl.pallas_call(
        functools.partial(_hgrn_body, n_heads=hb, t_chunk=tc, t_valid=t_valid, has_s0=has_s0,
                          n_side=len(cast_jobs)),
        out_shape=[jax.ShapeDtypeStruct((bsz, t, w), BF16),
                   jax.ShapeDtypeStruct((bsz, heads, d, d), F32)] + side[2],
        grid=grid,
        in_specs=in_specs + side[0],
        out_specs=[tok, state_spec] + side[1],
        scratch_shapes=[pltpu.VMEM((hb, d, d), F32), pltpu.VMEM((tc, hb * d), F32),
                        pltpu.VMEM((tc, hb * d), F32)],
        compiler_params=_cparams(("arbitrary", "arbitrary", "arbitrary")),
        name="hgrn2",
    )(*args, *side[3])
    return outs[0], outs[1], list(outs[2:])


def _memattn_body(q_ref, k_ref, v_ref, o_ref, *, n_heads, head_dim):
    scale = head_dim ** -0.5
    lanes = [slice(h * head_dim, (h + 1) * head_dim) for h in range(n_heads)]
    s = [lax.dot_general(q_ref[0, :, ls].astype(BF16), k_ref[0, :, ls].astype(BF16), NT_DIMS,
                         preferred_element_type=F32) * scale for ls in lanes]
    p = [jnp.exp(x - jnp.max(x, axis=-1, keepdims=True)) for x in s]
    p = [x / jnp.sum(x, axis=-1, keepdims=True) for x in p]
    o = [jnp.dot(x.astype(BF16), v_ref[0, :, ls].astype(BF16), preferred_element_type=F32)
         for x, ls in zip(p, lanes)]
    for x, ls in zip(o, lanes):
        o_ref[0, :, ls] = x.astype(o_ref.dtype)


def memory_attention(qm, mem_k, mem_v, layer, n_heads):
    bsz, t, w = qm.shape
    m = mem_k.shape[2]
    tq = _tile(t, 256)
    kv_spec = pl.BlockSpec((None, 1, m, w), lambda b, i: (layer, b, 0, 0))
    blk = pl.BlockSpec((1, tq, w), lambda b, i: (b, i, 0))
    return pl.pallas_call(
        functools.partial(_memattn_body, n_heads=n_heads, head_dim=w // n_heads),
        out_shape=jax.ShapeDtypeStruct((bsz, t, w), BF16),
        grid=(bsz, t // tq),
        in_specs=[blk, kv_spec, kv_spec],
        out_specs=blk,
        compiler_params=_cparams(("parallel", "parallel")),
        name="memory_attention",
    )(qm, mem_k, mem_v)


def _cumsum_body(x_ref, c_ref, ct_ref, carry_ref):
    @pl.when(pl.program_id(1) == 0)
    def _():
        carry_ref[...] = jnp.zeros_like(carry_ref)

    n = x_ref.shape[1]
    tri = (lax.broadcasted_iota(jnp.int32, (n, n), 0) >= lax.broadcasted_iota(jnp.int32, (n, n), 1)).astype(F32)
    c = jnp.dot(tri, x_ref[0], precision=HIGHEST, preferred_element_type=F32) + carry_ref[...]
    carry_ref[...] = c[n - 1:n, :]
    c_ref[0] = c
    ct_ref[0] = c.T


def cumsum_tokens(x):
    bsz, t, w = x.shape
    tc = _tile(t, 256)
    return pl.pallas_call(
        _cumsum_body,
        out_shape=[jax.ShapeDtypeStruct((bsz, t, w), F32), jax.ShapeDtypeStruct((bsz, w, t), F32)],
        grid=(bsz, t // tc),
        in_specs=[pl.BlockSpec((1, tc, w), lambda b, i: (b, i, 0))],
        out_specs=[pl.BlockSpec((1, tc, w), lambda b, i: (b, i, 0)),
                   pl.BlockSpec((1, w, tc), lambda b, i: (b, 0, i))],
        scratch_shapes=[pltpu.VMEM((1, w), F32)],
        compiler_params=_cparams(("parallel", "arbitrary")),
        name="cumsum_tokens",
    )(x)


def _fox_prompt_body(q_ref, k_ref, v_ref, c_ref, ct_ref, *refs, tq, tk, n_heads, n_side):
    side_in, refs = refs[:n_side], refs[n_side:]
    o_ref = refs[0]
    side_out = refs[1:1 + n_side]
    kb_ref, vb_ref, m_ref, l_ref, acc_ref = refs[1 + n_side:]
    _side_cast(side_in, side_out)
    hg = pl.program_id(1)
    i = pl.program_id(2)
    d = HEAD_DIM
    scale2 = d ** -0.5 * LOG2_E
    heads = range(n_heads)
    lanes = [slice(h * d, (h + 1) * d) for h in heads]

    @pl.when(i == 0)
    def _():
        kb_ref[...] = k_ref[0].astype(BF16)
        vb_ref[...] = v_ref[0].astype(BF16)

    q = (q_ref[0] * scale2).astype(BF16)
    lane = lax.broadcasted_iota(jnp.int32, (tq, LANES), 1)
    sub = lax.broadcasted_iota(jnp.int32, (SUBLANES, tk), 0)
    cq, ck_grp, ck_sel = [], [], []
    for h in heads:
        head = hg * n_heads + h
        cq.append(jnp.sum(jnp.where(lane == head, c_ref[0], 0.0), axis=1, keepdims=True) * LOG2_E)
        base = pl.multiple_of((head // SUBLANES) * SUBLANES, SUBLANES)
        ck_grp.append(pl.ds(base, SUBLANES))
        ck_sel.append(sub == (head - base))
    m_ref[...] = jnp.full_like(m_ref, -jnp.inf)
    l_ref[...] = jnp.zeros_like(l_ref)
    acc_ref[...] = jnp.zeros_like(acc_ref)

    n_full = (i * tq) // tk

    def tile(j, masked):
        ks = pl.ds(pl.multiple_of(j * tk, tk), tk)

        def logits(h):
            s = lax.dot_general(q[:, lanes[h]], kb_ref[ks, lanes[h]], NT_DIMS, preferred_element_type=F32)
            s = s + cq[h] - _select_row(ct_ref[0, ck_grp[h], ks], ck_sel[h]) * LOG2_E
            if masked:
                r = lax.broadcasted_iota(jnp.int32, (tq, tk), 0) + (i * tq - j * tk)
                cc = lax.broadcasted_iota(jnp.int32, (tq, tk), 1)
                s = jnp.where(cc <= r, s, -jnp.inf)
            return s

        def finish(h, s):
            m_old = m_ref[:, lanes[h]]
            m_new = jnp.maximum(m_old, jnp.max(s, axis=-1, keepdims=True))
            p = jnp.exp2(s - _lane_tiles(m_new, tk // LANES))
            alpha = jnp.exp2(m_old - m_new)
            pv = jnp.dot(p.astype(BF16), vb_ref[ks, lanes[h]], preferred_element_type=F32)
            l_ref[:, lanes[h]] = alpha * l_ref[:, lanes[h]] + jnp.sum(p, axis=-1, keepdims=True)
            acc_ref[:, lanes[h]] = alpha * acc_ref[:, lanes[h]] + pv
            m_ref[:, lanes[h]] = m_new

        s_prev = logits(0)
        for h in range(1, n_heads):
            s_cur = logits(h)
            finish(h - 1, s_prev)
            s_prev = s_cur
        finish(n_heads - 1, s_prev)

    def body(j, carry):
        tile(j, False)
        return carry

    lax.fori_loop(0, n_full, body, 0)
    tile(n_full, True)
    o_ref[0] = (acc_ref[...] / l_ref[...]).astype(o_ref.dtype)


def fox_attention_prompt(q, k, v, c, ct, cast_jobs=()):
    bsz, t, w = q.shape
    d = HEAD_DIM
    heads = w // d
    hb = 4 if heads % 4 == 0 else 1
    tq = _tile(t, 256)
    tk = _tile(t, 256)
    assert tk % tq == 0
    blk = pl.BlockSpec((1, tq, hb * d), lambda b, h, i: (b, i, h))
    full = pl.BlockSpec((1, t, hb * d), lambda b, h, i: (b, 0, h))
    grid = (bsz, heads // hb, t // tq)
    side = _side_cast_plan(cast_jobs, grid[0] * grid[1] * grid[2],
                           lambda b, h, i: (b * grid[1] + h) * grid[2] + i)
    outs = pl.pallas_call(
        functools.partial(_fox_prompt_body, tq=tq, tk=tk, n_heads=hb, n_side=len(cast_jobs)),
        out_shape=[jax.ShapeDtypeStruct((bsz, t, w), BF16)] + side[2],
        grid=grid,
        in_specs=[blk, full, full,
                  pl.BlockSpec((1, tq, LANES), lambda b, h, i: (b, i, 0)),
                  pl.BlockSpec((1, LANES, t), lambda b, h, i: (b, 0, 0))] + side[0],
        out_specs=[blk] + side[1],
        scratch_shapes=[pltpu.VMEM((t, hb * d), BF16), pltpu.VMEM((t, hb * d), BF16),
                        pltpu.VMEM((tq, hb * d), F32), pltpu.VMEM((tq, hb * d), F32),
                        pltpu.VMEM((tq, hb * d), F32)],
        compiler_params=_cparams(("arbitrary", "arbitrary", "arbitrary")),
        name="fox_attention_prompt",
    )(q, k, v, c, ct, *side[3])
    return outs[0], list(outs[1:])


def _suffix_body(pt_ref, *refs, pages_per_step):
    del pt_ref
    x_refs = refs[:pages_per_step]
    d_ref, carry_ref = refs[pages_per_step:]

    @pl.when(pl.program_id(1) == 0)
    def _():
        carry_ref[...] = jnp.zeros_like(carry_ref)

    n = x_refs[0].shape[2]
    later = (lax.broadcasted_iota(jnp.int32, (n, n), 0) > lax.broadcasted_iota(jnp.int32, (n, n), 1)).astype(F32)
    within = [jnp.dot(x_ref[0], later, precision=HIGHEST, preferred_element_type=F32) for x_ref in x_refs]
    carry = carry_ref[...]
    for r in range(pages_per_step):
        d_ref[0, pages_per_step - 1 - r] = within[r] + carry
        carry = carry + jnp.sum(x_refs[r][0], axis=1, keepdims=True)
    carry_ref[...] = carry


def suffix_log_forget(logf_t, page_table):
    bsz, n_pages = page_table.shape
    _, heads, page = logf_t.shape
    r_pages = 16 if n_pages % 16 == 0 else 1
    steps = n_pages // r_pages

    def page_spec(r):
        return pl.BlockSpec((1, heads, page), lambda b, p, pt: (pt[b, n_pages - 1 - (p * r_pages + r)], 0, 0))

    grid_spec = pltpu.PrefetchScalarGridSpec(
        num_scalar_prefetch=1,
        grid=(bsz, steps),
        in_specs=[page_spec(r) for r in range(r_pages)],
        out_specs=pl.BlockSpec((1, r_pages, heads, page), lambda b, p, pt: (b, steps - 1 - p, 0, 0)),
        scratch_shapes=[pltpu.VMEM((heads, page), F32)],
    )
    return pl.pallas_call(
        functools.partial(_suffix_body, pages_per_step=r_pages),
        out_shape=jax.ShapeDtypeStruct((bsz, n_pages, heads, page), F32),
        grid_spec=grid_spec,
        compiler_params=_cparams(("parallel", "arbitrary")),
        name="suffix_log_forget",
    )(page_table, *([logf_t] * r_pages))


def _fox_decode_body(pt_ref, q_ref, *rest, n_heads, pages_per_step, page):
    del pt_ref
    r_pages = pages_per_step
    k_refs = rest[:r_pages]
    v_refs = rest[r_pages:2 * r_pages]
    (d_ref, cn_ref, cnt_ref, kn_ref, vn_ref, o_ref, m_ref, l_ref, acc_ref) = rest[2 * r_pages:]
    p_idx = pl.program_id(1)
    d = HEAD_DIM
    tq = q_ref.shape[1]
    scale = d ** -0.5
    half = n_heads // 2
    pairs = range(half)
    lanes = [slice(h * d, (h + 1) * d) for h in range(n_heads)]
    n_il = 2 * page

    @pl.when(p_idx == 0)
    def _():
        m_ref[...] = jnp.full_like(m_ref, -jnp.inf)
        l_ref[...] = jnp.zeros_like(l_ref)
        acc_ref[...] = jnp.zeros_like(acc_ref)

    def stack(top, bottom):
        return jnp.concatenate([top, bottom], axis=0)

    def update(s, pv_of):
        m_old = [m_ref[:, lanes[p]] for p in pairs]
        m_new = [jnp.maximum(m_old[p], jnp.max(s[p], axis=-1, keepdims=True)) for p in pairs]
        pr = [jnp.exp(s[p] - _lane_tiles(m_new[p], s[p].shape[1] // LANES)) for p in pairs]
        alpha = [jnp.exp(m_old[p] - m_new[p]) for p in pairs]
        pv = [pv_of(p, pr[p].astype(BF16)) for p in pairs]
        for p in pairs:
            l_ref[:, lanes[p]] = alpha[p] * l_ref[:, lanes[p]] + jnp.sum(pr[p], axis=-1, keepdims=True)
            acc_ref[:, lanes[p]] = alpha[p] * acc_ref[:, lanes[p]] + pv[p]
            m_ref[:, lanes[p]] = m_new[p]

    q = q_ref[0]
    qp = [stack(q[:, lanes[p]], q[:, lanes[p + half]]).astype(BF16) for p in pairs]
    cq = [stack(cn_ref[0, :, p:p + 1], cn_ref[0, :, p + half:p + half + 1]) for p in pairs]

    def pair_rows(ref, p):
        return ref[0, pl.ds(p, n_il, stride=half), :].astype(BF16)

    n_keys = r_pages * n_il
    row = lax.broadcasted_iota(jnp.int32, (2 * tq, n_keys), 0)
    col = lax.broadcasted_iota(jnp.int32, (2 * tq, n_keys), 1)
    own = (col & 1) == (row >= tq).astype(jnp.int32)

    s = []
    for p in pairs:
        parts = [lax.dot_general(qp[p], pair_rows(k_refs[r], p), NT_DIMS, preferred_element_type=F32) * scale
                 + (cq[p] + d_ref[0, r, p:p + 1, :]) for r in range(r_pages)]
        sp = parts[0] if r_pages == 1 else jnp.concatenate(parts, axis=1)
        s.append(jnp.where(own, sp, -jnp.inf))

    def pv_past(p, pb):
        acc = jnp.dot(pb[:, 0:n_il], pair_rows(v_refs[0], p), preferred_element_type=F32)
        for r in range(1, r_pages):
            acc = acc + jnp.dot(pb[:, r * n_il:(r + 1) * n_il], pair_rows(v_refs[r], p),
                                preferred_element_type=F32)
        return acc

    update(s, pv_past)

    @pl.when(p_idx == pl.num_programs(1) - 1)
    def _():
        n_new = kn_ref.shape[1]
        r_i = lax.broadcasted_iota(jnp.int32, (tq, n_new), 0)
        c_i = lax.broadcasted_iota(jnp.int32, (tq, n_new), 1)
        causal = stack(c_i <= r_i, c_i <= r_i)

        def logits_new(p):
            top = lax.dot_general(qp[p], kn_ref[0, :, lanes[p]].astype(BF16), NT_DIMS,
                                  preferred_element_type=F32)[:tq]
            bot = lax.dot_general(qp[p], kn_ref[0, :, lanes[p + half]].astype(BF16), NT_DIMS,
                                  preferred_element_type=F32)[tq:]
            bias = stack(cn_ref[0, :, p:p + 1] - cnt_ref[0, p:p + 1, :],
                         cn_ref[0, :, p + half:p + half + 1] - cnt_ref[0, p + half:p + half + 1, :])
            return jnp.where(causal, stack(top, bot) * scale + bias, -jnp.inf)

        def pv_new(p, pb):
            top = jnp.dot(pb, vn_ref[0, :, lanes[p]].astype(BF16), preferred_element_type=F32)[:tq]
            bot = jnp.dot(pb, vn_ref[0, :, lanes[p + half]].astype(BF16), preferred_element_type=F32)[tq:]
            return stack(top, bot)

        update([logits_new(p) for p in pairs], pv_new)
        o = acc_ref[...] / l_ref[...]
        for p in pairs:
            o_ref[0, :, lanes[p]] = o[:tq, lanes[p]].astype(o_ref.dtype)
            o_ref[0, :, lanes[p + half]] = o[tq:, lanes[p]].astype(o_ref.dtype)


def fox_attention_decode(q, cache_k, cache_v, d_suffix, cn, cnt, k_new, v_new, page_table):
    bsz, tq, w = q.shape
    heads = w // HEAD_DIM
    assert heads % 2 == 0
    half = heads // 2
    rows = cache_k.shape[1]
    page = rows // heads
    n_pages = page_table.shape[1]
    tn = k_new.shape[1]
    r_pages = 4 if n_pages % 4 == 0 else 1
    steps = n_pages // r_pages
    d_pairs = d_suffix.reshape(bsz, n_pages, 2, half, page).transpose(0, 1, 3, 4, 2).reshape(
        bsz, n_pages, half, 2 * page)

    def page_spec(r):
        return pl.BlockSpec((1, rows, HEAD_DIM), lambda b, p, pt: (pt[b, p * r_pages + r], 0, 0))

    whole = lambda shape: pl.BlockSpec((1,) + shape, lambda b, p, pt: (b, 0, 0))
    grid_spec = pltpu.PrefetchScalarGridSpec(
        num_scalar_prefetch=1,
        grid=(bsz, steps),
        in_specs=[whole((tq, w))] + [page_spec(r) for r in range(r_pages)] * 2
                 + [pl.BlockSpec((1, r_pages, half, 2 * page), lambda b, p, pt: (b, p, 0, 0)),
                    pl.BlockSpec((1, tq, LANES), lambda b, p, pt: (b, 0, 0)),
                    whole((LANES, tn)), whole((tn, w)), whole((tn, w))],
        out_specs=pl.BlockSpec((1, tq, w), lambda b, p, pt: (b, 0, 0)),
        scratch_shapes=[pltpu.VMEM((2 * tq, w // 2), F32)] * 3,
    )
    return pl.pallas_call(
        functools.partial(_fox_decode_body, n_heads=heads, pages_per_step=r_pages, page=page),
        out_shape=jax.ShapeDtypeStruct((bsz, tq, w), BF16),
        grid_spec=grid_spec,
        compiler_params=_cparams(("parallel", "arbitrary")),
        name="fox_attention_decode",
    )(page_table, q, *([cache_k] * r_pages), *([cache_v] * r_pages), d_pairs, cn, cnt, k_new, v_new)


def _pad_rows(x, rows):
    pad = [(0, 0)] * x.ndim
    pad[1] = (0, rows - x.shape[1])
    return jnp.pad(x, pad)


def _trunk(hp, hs, dims_p, dims_s, mem_p, mem_s, hg_s0, decode, wts):
    d_model = hp.shape[1]
    tok_w = wts["tok_w"]
    mem_w = d_model - tok_w
    mem_heads = mem_p[0].shape[3]
    to4 = lambda a: a.reshape(a.shape[:3] + (mem_w,))
    mem_p = (to4(mem_p[0]), to4(mem_p[1]))
    mem_s = (to4(mem_s[0]), to4(mem_s[1]))
    n_a = wts["w_in_a"].shape[0]
    depth = wts["w_out"].shape[0]
    (bp, tp), (bs, ts) = dims_p, dims_s
    states_p, states_s = [], []
    hn_p = rmsnorm_bf16(hp, wts["norm_pre_mix"][0])
    hn_s = rmsnorm_bf16(hs, wts["norm_pre_mix"][0])
    kv_p = kv_s = None
    hkv_p = hkv_s = None
    for layer in range(depth):
        cast_jobs = [(wts["w_out"], layer), (wts["w_mlp_up"], layer), (wts["w_mlp_down"], layer)]
        if layer == n_a:
            wkv = wts["w_kv"]
            k_p, k_s = matmul(hkv_p, hkv_s, wkv, 0, tok_w, trans_w=True)
            v_p, v_s = matmul(hkv_p, hkv_s, wkv, tok_w, tok_w, trans_w=True)
            lf_p, lf_s = matmul(hkv_p, hkv_s, wts["w_fg"], 0, LANES, kind="logsig", vecs=(wts["b_fg"],),
                                trans_w=True)
            kv_p, kv_s = (k_p, v_p, lf_p), (k_s, v_s, lf_s)
        if layer < n_a:
            w_in = wts["w_in_a"]
            q = matmul(hn_p, hn_s, w_in, 0, tok_w, kind="silu", layer=layer)
            f = matmul(hn_p, hn_s, w_in, tok_w, tok_w, kind="fgate", vecs=(wts["lb"][layer],), layer=layer)
            v = matmul(hn_p, hn_s, w_in, 2 * tok_w, tok_w, layer=layer)
            g = matmul(hn_p, hn_s, w_in, 3 * tok_w, tok_w, kind="silu", layer=layer)
            zm_p, zm_s = matmul(hn_p, hn_s, w_in, 4 * tok_w, mem_w, layer=layer)
            gain = wts["hg_out_norm"][layer]
            shp_p, shp_s = (bp, tp, tok_w), (bs, ts, tok_w)
            o_p, st_p, w_bf = hgrn2(q[0].reshape(shp_p), f[0].reshape(shp_p), v[0].reshape(shp_p),
                                    g[0].reshape(shp_p), gain, None, cast_jobs=cast_jobs)
            o_s, st_s, _ = hgrn2(q[1].reshape(shp_s), f[1].reshape(shp_s), v[1].reshape(shp_s), g[1].reshape(shp_s),
                                 gain, hg_s0[layer], wts["t_valid"])
            states_p.append(st_p)
            states_s.append(st_s)
        else:
            w_in = wts["w_in_b"]
            q_p, q_s = matmul(hn_p, hn_s, w_in, 0, tok_w, layer=layer - n_a)
            zm_p, zm_s = matmul(hn_p, hn_s, w_in, tok_w, mem_w, layer=layer - n_a)
            c, ct = cumsum_tokens(kv_p[2].reshape(bp, tp, LANES))
            o_p, w_bf = fox_attention_prompt(q_p.reshape(bp, tp, tok_w), kv_p[0].reshape(bp, tp, tok_w),
                                             kv_p[1].reshape(bp, tp, tok_w), c, ct, cast_jobs=cast_jobs)
            cn, cnt = cumsum_tokens(_pad_rows(kv_s[2].reshape(bs, ts, LANES), LANES))
            kn = _pad_rows(kv_s[0].reshape(bs, ts, tok_w), LANES)
            vn = _pad_rows(kv_s[1].reshape(bs, ts, tok_w), LANES)
            o_s = fox_attention_decode(q_s.reshape(bs, ts, tok_w), decode["cache_k"], decode["cache_v"],
                                       decode["d_suffix"], cn, cnt, kn, vn, decode["page_table"])
        om_p = memory_attention(zm_p.reshape(bp, tp, mem_w), mem_p[0], mem_p[1], layer, mem_heads)
        om_s = memory_attention(zm_s.reshape(bs, ts, mem_w), mem_s[0], mem_s[1], layer, mem_heads)
        w_out_b, w_up_b, w_down_b = [w[None] for w in w_bf]
        mixed_p, mixed_s = matmul((o_p.reshape(bp * tp, tok_w), om_p.reshape(bp * tp, mem_w)),
                                  (o_s.reshape(bs * ts, tok_w), om_s.reshape(bs * ts, mem_w)),
                                  w_out_b, 0, d_model)
        hp, (hn_p,) = post_norm_residual(mixed_p, hp, wts["norm_post_mix"][layer], [wts["norm_pre_mlp"][layer]])
        hs, (hn_s,) = post_norm_residual(mixed_s, hs, wts["norm_post_mix"][layer], [wts["norm_pre_mlp"][layer]])
        u_p, u_s = matmul(hn_p, hn_s, w_up_b, 0, w_up_b.shape[2], kind="relu2", out_dtype=BF16)
        down_p, down_s = matmul_long_k(u_p, u_s, w_down_b, 0)
        nxt = []
        if layer + 1 < depth:
            nxt = [wts["norm_pre_mix"][layer + 1]] + ([wts["kv_norm"]] if layer + 1 == n_a else [])
        hp, hns_p = post_norm_residual(down_p, hp, wts["norm_post_mlp"][layer], nxt)
        hs, hns_s = post_norm_residual(down_s, hs, wts["norm_post_mlp"][layer], nxt)
        if nxt:
            hn_p, hn_s = hns_p[0], hns_s[0]
            if layer + 1 == n_a:
                hkv_p, hkv_s = hns_p[1], hns_s[1]
    return (hp, jnp.stack(states_p), kv_p), (hs, jnp.stack(states_s), kv_s)


def kernel(x_prompt, x_sample, state_hgrn, cache_k, cache_v, cache_logf, cache_mem_k, cache_mem_v, page_table, mem_prompt, norm_pre_mix, norm_post_mix, norm_pre_mlp, norm_post_mlp, w_in_a, hg_lb_logits, hg_out_norm, w_in_b, kv_norm, w_kv_shared, b_fgate, w_out, mem_norm, w_mem_kv, w_mlp_up, w_mlp_down):
    bp, tp, d_model = x_prompt.shape
    bs, ts, _ = x_sample.shape
    fox_heads, fox_dim = cache_k.shape[2], cache_k.shape[3]
    assert fox_dim == HEAD_DIM and state_hgrn.shape[-1] == HEAD_DIM
    tok_w = fox_heads * fox_dim
    mem_w = d_model - tok_w
    mem_len = mem_prompt.shape[1]
    depth = w_out.shape[0]
    n_pool, page = cache_k.shape[0], cache_k.shape[1]
    t_pad = max(SUBLANES, ts)

    lb_all = jnp.cumsum(jax.nn.softmax(hg_lb_logits.astype(F32), axis=0), axis=0)
    wts = {
        "tok_w": tok_w, "t_valid": ts,
        "norm_pre_mix": norm_pre_mix, "norm_post_mix": norm_post_mix,
        "norm_pre_mlp": norm_pre_mlp, "norm_post_mlp": norm_post_mlp,
        "kv_norm": kv_norm, "hg_out_norm": hg_out_norm, "lb": lb_all,
        "w_in_a": w_in_a, "w_in_b": w_in_b,
        "w_kv": w_kv_shared.T[None],
        "w_fg": jnp.pad(w_kv_shared.T[2 * tok_w:], ((0, LANES - fox_heads), (0, 0)))[None],
        "b_fg": jnp.pad(b_fgate.astype(F32), (0, LANES - fox_heads)),
        "w_out": w_out, "w_mlp_up": w_mlp_up, "w_mlp_down": w_mlp_down,
    }

    mem_n = rmsnorm_bf16(mem_prompt.reshape(bp * mem_len, d_model), mem_norm)
    mem_k_flat = matmul(mem_n, None, w_mem_kv, 0, mem_w, all_layers=True)
    mem_v_flat = matmul(mem_n, None, w_mem_kv, mem_w, mem_w, all_layers=True)
    mem_heads = cache_mem_k.shape[3]
    mem_shape = (depth, bp, mem_len, mem_heads, mem_w // mem_heads)
    mem_k_prompt = mem_k_flat.reshape(mem_shape)
    mem_v_prompt = mem_v_flat.reshape(mem_shape)

    decode = {
        "cache_k": cache_k.reshape(n_pool, page * fox_heads, fox_dim),
        "cache_v": cache_v.reshape(n_pool, page * fox_heads, fox_dim),
        "d_suffix": suffix_log_forget(cache_logf.transpose(0, 2, 1), page_table),
        "page_table": page_table,
    }
    xs = _pad_rows(x_sample, t_pad).reshape(bs * t_pad, d_model)
    (y_p, hg_p, (k_p, v_p, lf_p)), (y_s, hg_s, (k_s, v_s, lf_s)) = _trunk(
        x_prompt.reshape(bp * tp, d_model), xs, (bp, tp), (bs, t_pad),
        (mem_k_prompt, mem_v_prompt), (cache_mem_k, cache_mem_v), state_hgrn, decode, wts)

    def unpad(a, width):
        return a.reshape(bs, t_pad, width)[:, :ts]

    return (y_p.reshape(bp, tp, d_model),
            unpad(y_s, d_model),
            hg_p.astype(state_hgrn.dtype),
            hg_s.astype(state_hgrn.dtype),
            k_p.reshape(bp, tp, fox_heads, fox_dim),
            v_p.reshape(bp, tp, fox_heads, fox_dim),
            lf_p[:, :fox_heads].reshape(bp, tp, fox_heads),
            unpad(k_s, tok_w).reshape(bs, ts, fox_heads, fox_dim),
            unpad(v_s, tok_w).reshape(bs, ts, fox_heads, fox_dim),
            unpad(lf_s, LANES)[:, :, :fox_heads],
            mem_k_prompt, mem_v_prompt)
```

```python
import functools

import jax
import jax.numpy as jnp
from jax import lax
from jax.experimental import pallas as pl
from jax.experimental.pallas import tpu as pltpu

F32 = jnp.float32
BF16 = jnp.bfloat16

NORM_EPS = 1e-6
HEAD_DIM = 128
LANES = 128
SUBLANES = 8
VMEM_LIMIT = 52 * 1024 * 1024
HG_CHUNK = 64
HG_SAFE_DECAY = 60.0
HIGHEST = lax.Precision.HIGHEST
LOG2_E = 1.4426950408889634

NT_DIMS = (((1,), (1,)), ((), ()))
TN_DIMS = (((0,), (0,)), ((), ()))


def _cparams(semantics):
    return pltpu.CompilerParams(dimension_semantics=semantics, vmem_limit_bytes=VMEM_LIMIT)


def _tile(n, want):
    if n <= want:
        return n
    t = want
    while t >= 16:
        if n % t == 0:
            return t
        t -= 16
    return n


def _lane_tiles(x, n):
    return x if n == 1 else jnp.concatenate([x] * n, axis=1)


def _cumsum_rows(tri, x):
    w = x.shape[1]
    hi = x.astype(BF16)
    rest = x - hi.astype(F32)
    mid = rest.astype(BF16)
    lo = (rest - mid.astype(F32)).astype(BF16)
    y = jnp.dot(tri, jnp.concatenate([hi, mid, lo], axis=1), preferred_element_type=F32)
    return y[:, :w] + y[:, w:2 * w] + y[:, 2 * w:]


def _select_row(group, sel):
    return jnp.sum(jnp.where(sel, group, 0.0), axis=0, keepdims=True)


def _rms_scale(x):
    return lax.rsqrt(jnp.mean(x * x, axis=-1, keepdims=True) + NORM_EPS)


def _rmsnorm_body(x_ref, g_ref, o_ref):
    x = x_ref[...]
    o_ref[...] = (x * _rms_scale(x) * g_ref[...]).astype(o_ref.dtype)


def rmsnorm_bf16(x, gain):
    m, d = x.shape
    tm = _tile(m, 256)
    return pl.pallas_call(
        _rmsnorm_body,
        out_shape=jax.ShapeDtypeStruct((m, d), BF16),
        grid=(m // tm,),
        in_specs=[pl.BlockSpec((tm, d), lambda i: (i, 0)), pl.BlockSpec((1, d), lambda i: (0, 0))],
        out_specs=pl.BlockSpec((tm, d), lambda i: (i, 0)),
        compiler_params=_cparams(("parallel",)),
        name="rmsnorm",
    )(x, gain.reshape(1, d))


def _post_body(m_ref, h_ref, gp_ref, *rest, n_next):
    g_next = rest[:n_next]
    ho_ref = rest[n_next]
    hn_refs = rest[n_next + 1:]
    m = m_ref[...]
    h = h_ref[...] + m * _rms_scale(m) * gp_ref[...]
    ho_ref[...] = h
    if n_next:
        hs = h * _rms_scale(h)
        for g_ref, hn_ref in zip(g_next, hn_refs):
            hn_ref[...] = (hs * g_ref[...]).astype(hn_ref.dtype)


def post_norm_residual(mixed, h, gain_post, next_gains):
    m, d = h.shape
    tm = _tile(m, 256)
    n_next = len(next_gains)
    row = pl.BlockSpec((tm, d), lambda i: (i, 0))
    vec = pl.BlockSpec((1, d), lambda i: (0, 0))
    outs = pl.pallas_call(
        functools.partial(_post_body, n_next=n_next),
        out_shape=[jax.ShapeDtypeStruct((m, d), F32)] + [jax.ShapeDtypeStruct((m, d), BF16)] * n_next,
        grid=(m // tm,),
        in_specs=[row, row, vec] + [vec] * n_next,
        out_specs=[row] * (1 + n_next),
        compiler_params=_cparams(("parallel",)),
        name="post_norm_residual",
    )(mixed, h, gain_post.reshape(1, d), *[g.reshape(1, d) for g in next_gains])
    return outs[0], list(outs[1:])


def _epilogue(kind, acc, vecs):
    if kind == "none":
        return acc
    if kind == "silu":
        return acc * jax.nn.sigmoid(acc)
    if kind == "fgate":
        lb = vecs[0][...]
        return lb + (1.0 - lb) * jax.nn.sigmoid(acc)
    if kind == "relu2":
        r = jnp.maximum(acc, 0.0)
        return r * r
    if kind == "logsig":
        return jax.nn.log_sigmoid(acc + vecs[0][...])
    raise ValueError(kind)


def _mm_body(*refs, kind, n_vec, n_seg, has_second, cast_w, trans_w):
    x_refs, refs = refs[:n_seg], refs[n_seg:]
    x2_refs = ()
    if has_second:
        x2_refs, refs = refs[:n_seg], refs[n_seg:]
    w_ref, rest = refs[0], refs[1:]
    vecs = rest[:n_vec]
    outs = rest[n_vec:]
    o_ref = outs[0]
    i = pl.program_id(2)
    if cast_w:
        wb_ref = outs[-1]

        @pl.when(i == 0)
        def _():
            wb_ref[...] = w_ref[...].astype(BF16)
    else:
        wb_ref = w_ref

    def mm(a_refs):
        acc, off = None, 0
        for a_ref in a_refs:
            ks = a_ref.shape[1]
            if trans_w:
                part = lax.dot_general(a_ref[...], wb_ref[:, off:off + ks], NT_DIMS, preferred_element_type=F32)
            else:
                part = jnp.dot(a_ref[...], wb_ref[off:off + ks, :], preferred_element_type=F32)
            acc = part if acc is None else acc + part
            off += ks
        return acc

    o_ref[...] = _epilogue(kind, mm(x_refs), vecs).astype(o_ref.dtype)

    if has_second:
        o2_ref = outs[1]

        @pl.when(i == pl.num_programs(2) - 1)
        def _():
            o2_ref[...] = _epilogue(kind, mm(x2_refs), vecs).astype(o2_ref.dtype)


def matmul(x, x2, w, col_off, n_cols, kind="none", vecs=(), out_dtype=F32, layer=0, all_layers=False,
           trans_w=False):
    xs = x if isinstance(x, tuple) else (x,)
    m = xs[0].shape[0]
    k = sum(a.shape[1] for a in xs)
    nl = w.shape[0] if all_layers else 1
    cast_w = w.dtype != BF16
    tm = _tile(m, 1024)
    tn = _tile(n_cols, 512 if cast_w else 1024)
    assert col_off % tn == 0 and n_cols % tn == 0
    jo = col_off // tn
    has_second = x2 is not None
    vec_arrs = [v.reshape(1, n_cols).astype(F32) for v in vecs]
    in_specs = [pl.BlockSpec((tm, a.shape[1]), lambda l, j, i: (i, 0)) for a in xs]
    args = list(xs)
    out_shape = [jax.ShapeDtypeStruct((nl, m, n_cols), out_dtype)]
    out_specs = [pl.BlockSpec((None, tm, tn), lambda l, j, i: (l, i, j))]
    if has_second:
        x2s = x2 if isinstance(x2, tuple) else (x2,)
        assert [a.shape[1] for a in x2s] == [a.shape[1] for a in xs]
        m2 = x2s[0].shape[0]
        in_specs += [pl.BlockSpec((m2, a.shape[1]), lambda l, j, i: (0, 0)) for a in x2s]
        args += list(x2s)
        out_shape.append(jax.ShapeDtypeStruct((nl, m2, n_cols), out_dtype))
        out_specs.append(pl.BlockSpec((None, m2, tn), lambda l, j, i: (l, 0, j)))
    w_layer = lambda l: l if all_layers else layer
    if trans_w:
        w_block = (tn, k)
        in_specs.append(pl.BlockSpec((None, tn, k), lambda l, j, i: (w_layer(l), jo + j, 0)))
    else:
        w_block = (k, tn)
        in_specs.append(pl.BlockSpec((None, k, tn), lambda l, j, i: (w_layer(l), 0, jo + j)))
    in_specs += [pl.BlockSpec((1, tn), lambda l, j, i: (0, j))] * len(vec_arrs)
    outs = pl.pallas_call(
        functools.partial(_mm_body, kind=kind, n_vec=len(vec_arrs), n_seg=len(xs), has_second=has_second,
                          cast_w=cast_w, trans_w=trans_w),
        out_shape=out_shape,
        grid=(nl, n_cols // tn, m // tm),
        in_specs=in_specs,
        out_specs=out_specs,
        scratch_shapes=[pltpu.VMEM(w_block, BF16)] if cast_w else [],
        compiler_params=_cparams(("parallel", "parallel", "arbitrary")),
        name="matmul_" + kind,
    )(*args, w, *vec_arrs)
    if all_layers:
        return outs[0]
    return (outs[0][0], outs[1][0]) if has_second else outs[0][0]


def _mm_acc_body(x_ref, x2_ref, w_ref, o_ref, o2_ref):
    i = pl.program_id(1)
    kk = pl.program_id(2)

    @pl.when(kk == 0)
    def _():
        o_ref[...] = jnp.zeros_like(o_ref)

    o_ref[...] += jnp.dot(x_ref[...], w_ref[...], preferred_element_type=F32)

    first = i == 0

    @pl.when(jnp.logical_and(first, kk == 0))
    def _():
        o2_ref[...] = jnp.zeros_like(o2_ref)

    @pl.when(first)
    def _():
        o2_ref[...] += jnp.dot(x2_ref[...], w_ref[...], preferred_element_type=F32)


def matmul_long_k(x, x2, w, layer):
    m, k = x.shape
    m2 = x2.shape[0]
    n = w.shape[2]
    tm = _tile(m, 1024)
    tn = _tile(n, 1024)
    tk = _tile(k, 4096)
    return pl.pallas_call(
        _mm_acc_body,
        out_shape=[jax.ShapeDtypeStruct((m, n), F32), jax.ShapeDtypeStruct((m2, n), F32)],
        grid=(n // tn, m // tm, k // tk),
        in_specs=[pl.BlockSpec((tm, tk), lambda j, i, kk: (i, kk)),
                  pl.BlockSpec((m2, tk), lambda j, i, kk: (0, kk)),
                  pl.BlockSpec((None, tk, tn), lambda j, i, kk: (layer, kk, j))],
        out_specs=[pl.BlockSpec((tm, tn), lambda j, i, kk: (i, j)),
                   pl.BlockSpec((m2, tn), lambda j, i, kk: (0, j))],
        compiler_params=_cparams(("parallel", "arbitrary", "arbitrary")),
        name="matmul_long_k",
    )(x, x2, w)


def _side_cast_plan(jobs, n_steps, step_of):
    in_specs, out_specs, out_shapes, arrays = [], [], [], []
    for w, layer in jobs:
        _, k, c = w.shape
        rows = next(r for r in range(16, k + 1, 16) if k % r == 0 and k // r <= n_steps)
        last = k // rows - 1
        in_specs.append(pl.BlockSpec(
            (None, rows, c), lambda *g, layer=layer, last=last: (layer, jnp.minimum(step_of(*g), last), 0)))
        out_specs.append(pl.BlockSpec((rows, c), lambda *g, last=last: (jnp.minimum(step_of(*g), last), 0)))
        out_shapes.append(jax.ShapeDtypeStruct((k, c), BF16))
        arrays.append(w)
    return in_specs, out_specs, out_shapes, arrays


def _side_cast(src_refs, dst_refs):
    for src, dst in zip(src_refs, dst_refs):
        dst[...] = src[...].astype(BF16)


def _hgrn_body(*refs, n_heads, t_chunk, t_valid, has_s0, n_side):
    q_ref, f_ref, v_ref, g_ref, gain_ref = refs[:5]
    refs = refs[5:]
    s0_ref = None
    if has_s0:
        s0_ref, refs = refs[0], refs[1:]
    side_in, refs = refs[:n_side], refs[n_side:]
    o_ref, sout_ref = refs[:2]
    side_out = refs[2:2 + n_side]
    st_ref, o_scr, lg_scr = refs[2 + n_side:]
    _side_cast(side_in, side_out)
    t_idx = pl.program_id(2)
    d = HEAD_DIM
    c = min(HG_CHUNK, t_chunk)
    n_sub = t_chunk // c
    heads = range(n_heads)
    lanes = [slice(h * d, (h + 1) * d) for h in heads]

    @pl.when(t_idx == 0)
    def _():
        for h in heads:
            if has_s0:
                st_ref[h] = s0_ref[0, h].T
            else:
                st_ref[h] = jnp.zeros((d, d), F32)

    if t_valid is not None:
        assert n_sub == 1 and t_valid <= t_chunk
        o_scr[...] = jnp.zeros_like(o_scr)
        sts = [st_ref[h] for h in heads]
        for t in range(t_valid):
            f_t = f_ref[0, t:t + 1, :]
            v_t = v_ref[0, t:t + 1, :]
            q_t = q_ref[0, t:t + 1, :]
            v_cols = [jnp.broadcast_to(v_t[:, lanes[h]], (d, d)).T for h in heads]
            sts = [sts[h] * f_t[:, lanes[h]] + v_cols[h] * (1.0 - f_t[:, lanes[h]]) for h in heads]
            q8s = [jnp.broadcast_to(q_t[:, lanes[h]], (SUBLANES, d)).astype(BF16) for h in heads]
            o8s = [lax.dot_general(q8s[h], sts[h].astype(BF16), NT_DIMS, preferred_element_type=F32) for h in heads]
            for h in heads:
                o_scr[t:t + 1, lanes[h]] = o8s[h][0:1, :]
        for h in heads:
            st_ref[h] = sts[h]

    def token_steps(h, first, count):
        hs = pl.ds(h * d, d)

        def step(t, st):
            base = pl.multiple_of(((first + t) // SUBLANES) * SUBLANES, SUBLANES)
            grp = pl.ds(base, SUBLANES)
            sel = lax.broadcasted_iota(jnp.int32, (SUBLANES, d), 0) == (first + t - base)
            f_t = _select_row(f_ref[0, grp, hs], sel)
            v_col = jnp.broadcast_to(_select_row(v_ref[0, grp, hs], sel), (d, d)).T
            st = st * f_t + v_col * (1.0 - f_t)
            q8 = jnp.broadcast_to(_select_row(q_ref[0, grp, hs], sel), (SUBLANES, d)).astype(BF16)
            o8 = lax.dot_general(q8, st.astype(BF16), NT_DIMS, preferred_element_type=F32)
            o_scr[grp, hs] = jnp.where(sel, o8, o_scr[grp, hs])
            return st

        st_ref[h] = lax.fori_loop(0, count, step, st_ref[h])

    row = lax.broadcasted_iota(jnp.int32, (c, c), 0)
    col = lax.broadcasted_iota(jnp.int32, (c, c), 1)
    causal = row >= col
    tri = causal.astype(BF16)

    def chunk_steps(sc):
        rows = pl.ds(sc * c, c)
        b = _cumsum_rows(tri, lg_scr[rows, :])
        e_last = jnp.exp(b[c - 1:c, :])
        qt = (q_ref[0, rows, :] * jnp.exp(b)).astype(BF16)
        kt = (1.0 - f_ref[0, rows, :]) * jnp.exp(-b)
        kd = (kt * e_last).astype(BF16)
        kt = kt.astype(BF16)
        vb = v_ref[0, rows, :].astype(BF16)
        stb = [st_ref[h].astype(BF16) for h in heads]
        att = [lax.dot_general(qt[:, lanes[h]], kt[:, lanes[h]], NT_DIMS, preferred_element_type=F32) for h in heads]
        o_inter = [lax.dot_general(qt[:, lanes[h]], stb[h], NT_DIMS, preferred_element_type=F32) for h in heads]
        upd = [lax.dot_general(vb[:, lanes[h]], kd[:, lanes[h]], TN_DIMS, preferred_element_type=F32) for h in heads]
        att = [jnp.where(causal, a, 0.0).astype(BF16) for a in att]
        o_intra = [jnp.dot(att[h], vb[:, lanes[h]], preferred_element_type=F32) for h in heads]
        for h in heads:
            o_scr[rows, lanes[h]] = o_intra[h] + o_inter[h]
            st_ref[h] = st_ref[h] * e_last[:, lanes[h]] + upd[h]

    if t_valid is None:
        lg_scr[...] = jnp.log(f_ref[0])
        totals = [jnp.min(jnp.sum(lg_scr[pl.ds(sc * c, c), :], axis=0, keepdims=True)) for sc in range(n_sub)]
        safe = functools.reduce(jnp.minimum, totals) > -HG_SAFE_DECAY

        @pl.when(safe)
        def _():
            for sc in range(n_sub):
                chunk_steps(sc)

        @pl.when(jnp.logical_not(safe))
        def _():
            for h in heads:
                token_steps(h, 0, t_chunk)

    for h in heads:
        o = o_scr[:, lanes[h]]
        o = o * _rms_scale(o) * gain_ref[:, lanes[h]]
        o_ref[0, :, lanes[h]] = (o * g_ref[0, :, lanes[h]]).astype(o_ref.dtype)

    @pl.when(t_idx == pl.num_programs(2) - 1)
    def _():
        for h in heads:
            sout_ref[0, h] = st_ref[h].T


def hgrn2(q, f, v, g, gain, s0, t_valid=None, cast_jobs=()):
    bsz, t, w = q.shape
    d = HEAD_DIM
    heads = w // d
    hb = 6 if heads % 6 == 0 else (4 if heads % 4 == 0 else 1)
    tc = _tile(t, 512)
    has_s0 = s0 is not None
    tok = pl.BlockSpec((1, tc, hb * d), lambda b, hg, ti: (b, ti, hg))
    in_specs = [tok, tok, tok, tok, pl.BlockSpec((1, hb * d), lambda b, hg, ti: (0, hg))]
    args = [q, f, v, g, gain.reshape(1, w)]
    state_spec = pl.BlockSpec((1, hb, d, d), lambda b, hg, ti: (b, hg, 0, 0))
    if has_s0:
        in_specs.append(state_spec)
        args.append(s0)
    grid = (bsz, heads // hb, t // tc)
    side = _side_cast_plan(cast_jobs, grid[0] * grid[1] * grid[2],
                           lambda b, hg, ti: (b * grid[1] + hg) * grid[2] + ti)
    outs = pl.pallas_call(
        functools.partial(_hgrn_body, n_heads=hb, t_chunk=tc, t_valid=t_valid, has_s0=has_s0,
                          n_side=len(cast_jobs)),
        out_shape=[jax.ShapeDtypeStruct((bsz, t, w), BF16),
                   jax.ShapeDtypeStruct((bsz, heads, d, d), F32)] + side[2],
        grid=grid,
        in_specs=in_specs + side[0],
        out_specs=[tok, state_spec] + side[1],
        scratch_shapes=[pltpu.VMEM((hb, d, d), F32), pltpu.VMEM((tc, hb * d), F32),
                        pltpu.VMEM((tc, hb * d), F32)],
        compiler_params=_cparams(("arbitrary", "arbitrary", "arbitrary")),
        name="hgrn2",
    )(*args, *side[3])
    return outs[0], outs[1], list(outs[2:])


def _memattn_body(q_ref, k_ref, v_ref, o_ref, *, n_heads, head_dim):
    scale = head_dim ** -0.5
    lanes = [slice(h * head_dim, (h + 1) * head_dim) for h in range(n_heads)]
    s = [lax.dot_general(q_ref[0, :, ls].astype(BF16), k_ref[0, :, ls].astype(BF16), NT_DIMS,
                         preferred_element_type=F32) * scale for ls in lanes]
    p = [jnp.exp(x - jnp.max(x, axis=-1, keepdims=True)) for x in s]
    p = [x / jnp.sum(x, axis=-1, keepdims=True) for x in p]
    o = [jnp.dot(x.astype(BF16), v_ref[0, :, ls].astype(BF16), preferred_element_type=F32)
         for x, ls in zip(p, lanes)]
    for x, ls in zip(o, lanes):
        o_ref[0, :, ls] = x.astype(o_ref.dtype)


def memory_attention(qm, mem_k, mem_v, layer, n_heads):
    bsz, t, w = qm.shape
    m = mem_k.shape[2]
    tq = _tile(t, 512)
    kv_spec = pl.BlockSpec((None, 1, m, w), lambda b, i: (layer, b, 0, 0))
    blk = pl.BlockSpec((1, tq, w), lambda b, i: (b, i, 0))
    return pl.pallas_call(
        functools.partial(_memattn_body, n_heads=n_heads, head_dim=w // n_heads),
        out_shape=jax.ShapeDtypeStruct((bsz, t, w), BF16),
        grid=(bsz, t // tq),
        in_specs=[blk, kv_spec, kv_spec],
        out_specs=blk,
        compiler_params=_cparams(("parallel", "parallel")),
        name="memory_attention",
    )(qm, mem_k, mem_v)


def _cumsum_body(x_ref, c_ref, ct_ref, carry_ref):
    @pl.when(pl.program_id(1) == 0)
    def _():
        carry_ref[...] = jnp.zeros_like(carry_ref)

    n = x_ref.shape[1]
    tri = (lax.broadcasted_iota(jnp.int32, (n, n), 0) >= lax.broadcasted_iota(jnp.int32, (n, n), 1)).astype(F32)
    c = jnp.dot(tri, x_ref[0], precision=HIGHEST, preferred_element_type=F32) + carry_ref[...]
    carry_ref[...] = c[n - 1:n, :]
    c_ref[0] = c
    ct_ref[0] = c.T


def cumsum_tokens(x):
    bsz, t, w = x.shape
    tc = _tile(t, 256)
    return pl.pallas_call(
        _cumsum_body,
        out_shape=[jax.ShapeDtypeStruct((bsz, t, w), F32), jax.ShapeDtypeStruct((bsz, w, t), F32)],
        grid=(bsz, t // tc),
        in_specs=[pl.BlockSpec((1, tc, w), lambda b, i: (b, i, 0))],
        out_specs=[pl.BlockSpec((1, tc, w), lambda b, i: (b, i, 0)),
                   pl.BlockSpec((1, w, tc), lambda b, i: (b, 0, i))],
        scratch_shapes=[pltpu.VMEM((1, w), F32)],
        compiler_params=_cparams(("parallel", "arbitrary")),
        name="cumsum_tokens",
    )(x)


def _fox_prompt_body(q_ref, k_ref, v_ref, c_ref, ct_ref, *refs, tq, tk, n_heads, n_side):
    side_in, refs = refs[:n_side], refs[n_side:]
    o_ref = refs[0]
    side_out = refs[1:1 + n_side]
    kb_ref, vb_ref, m_ref, l_ref, acc_ref = refs[1 + n_side:]
    _side_cast(side_in, side_out)
    hg = pl.program_id(1)
    i = pl.program_id(2)
    d = HEAD_DIM
    scale2 = d ** -0.5 * LOG2_E
    heads = range(n_heads)
    lanes = [slice(h * d, (h + 1) * d) for h in heads]

    @pl.when(i == 0)
    def _():
        kb_ref[...] = k_ref[0].astype(BF16)
        vb_ref[...] = v_ref[0].astype(BF16)

    q = (q_ref[0] * scale2).astype(BF16)
    lane = lax.broadcasted_iota(jnp.int32, (tq, LANES), 1)
    sub = lax.broadcasted_iota(jnp.int32, (SUBLANES, tk), 0)
    cq, ck_grp, ck_sel = [], [], []
    for h in heads:
        head = hg * n_heads + h
        cq.append(jnp.sum(jnp.where(lane == head, c_ref[0], 0.0), axis=1, keepdims=True) * LOG2_E)
        base = pl.multiple_of((head // SUBLANES) * SUBLANES, SUBLANES)
        ck_grp.append(pl.ds(base, SUBLANES))
        ck_sel.append(sub == (head - base))
    m_ref[...] = jnp.full_like(m_ref, -jnp.inf)
    l_ref[...] = jnp.zeros_like(l_ref)
    acc_ref[...] = jnp.zeros_like(acc_ref)

    n_full = (i * tq) // tk

    def tile(j, masked):
        ks = pl.ds(pl.multiple_of(j * tk, tk), tk)

        def logits(h):
            s = lax.dot_general(q[:, lanes[h]], kb_ref[ks, lanes[h]], NT_DIMS, preferred_element_type=F32)
            s = s + cq[h] - _select_row(ct_ref[0, ck_grp[h], ks], ck_sel[h]) * LOG2_E
            if masked:
                r = lax.broadcasted_iota(jnp.int32, (tq, tk), 0) + (i * tq - j * tk)
                cc = lax.broadcasted_iota(jnp.int32, (tq, tk), 1)
                s = jnp.where(cc <= r, s, -jnp.inf)
            return s

        def finish(h, s):
            m_old = m_ref[:, lanes[h]]
            m_new = jnp.maximum(m_old, jnp.max(s, axis=-1, keepdims=True))
            p = jnp.exp2(s - _lane_tiles(m_new, tk // LANES))
            alpha = jnp.exp2(m_old - m_new)
            pv = jnp.dot(p.astype(BF16), vb_ref[ks, lanes[h]], preferred_element_type=F32)
            l_ref[:, lanes[h]] = alpha * l_ref[:, lanes[h]] + jnp.sum(p, axis=-1, keepdims=True)
            acc_ref[:, lanes[h]] = alpha * acc_ref[:, lanes[h]] + pv
            m_ref[:, lanes[h]] = m_new

        s_prev = logits(0)
        for h in range(1, n_heads):
            s_cur = logits(h)
            finish(h - 1, s_prev)
            s_prev = s_cur
        finish(n_heads - 1, s_prev)

    def body(j, carry):
        tile(j, False)
        return carry

    lax.fori_loop(0, n_full, body, 0)
    tile(n_full, True)
    o_ref[0] = (acc_ref[...] / l_ref[...]).astype(o_ref.dtype)


def fox_attention_prompt(q, k, v, c, ct, cast_jobs=()):
    bsz, t, w = q.shape
    d = HEAD_DIM
    heads = w // d
    hb = 4 if heads % 4 == 0 else 1
    tq = _tile(t, 256)
    tk = _tile(t, 256)
    assert tk % tq == 0
    blk = pl.BlockSpec((1, tq, hb * d), lambda b, h, i: (b, i, h))
    full = pl.BlockSpec((1, t, hb * d), lambda b, h, i: (b, 0, h))
    grid = (bsz, heads // hb, t // tq)
    side = _side_cast_plan(cast_jobs, grid[0] * grid[1] * grid[2],
                           lambda b, h, i: (b * grid[1] + h) * grid[2] + i)
    outs = pl.pallas_call(
        functools.partial(_fox_prompt_body, tq=tq, tk=tk, n_heads=hb, n_side=len(cast_jobs)),
        out_shape=[jax.ShapeDtypeStruct((bsz, t, w), BF16)] + side[2],
        grid=grid,
        in_specs=[blk, full, full,
                  pl.BlockSpec((1, tq, LANES), lambda b, h, i: (b, i, 0)),
                  pl.BlockSpec((1, LANES, t), lambda b, h, i: (b, 0, 0))] + side[0],
        out_specs=[blk] + side[1],
        scratch_shapes=[pltpu.VMEM((t, hb * d), BF16), pltpu.VMEM((t, hb * d), BF16),
                        pltpu.VMEM((tq, hb * d), F32), pltpu.VMEM((tq, hb * d), F32),
                        pltpu.VMEM((tq, hb * d), F32)],
        compiler_params=_cparams(("arbitrary", "arbitrary", "arbitrary")),
        name="fox_attention_prompt",
    )(q, k, v, c, ct, *side[3])
    return outs[0], list(outs[1:])


def _suffix_body(pt_ref, *refs, pages_per_step):
    del pt_ref
    x_refs = refs[:pages_per_step]
    d_ref, carry_ref = refs[pages_per_step:]

    @pl.when(pl.program_id(1) == 0)
    def _():
        carry_ref[...] = jnp.zeros_like(carry_ref)

    n = x_refs[0].shape[2]
    later = (lax.broadcasted_iota(jnp.int32, (n, n), 0) > lax.broadcasted_iota(jnp.int32, (n, n), 1)).astype(F32)
    within = [jnp.dot(x_ref[0], later, precision=HIGHEST, preferred_element_type=F32) for x_ref in x_refs]
    carry = carry_ref[...]
    for r in range(pages_per_step):
        d_ref[0, pages_per_step - 1 - r] = within[r] + carry
        carry = carry + jnp.sum(x_refs[r][0], axis=1, keepdims=True)
    carry_ref[...] = carry


def suffix_log_forget(logf_t, page_table):
    bsz, n_pages = page_table.shape
    _, heads, page = logf_t.shape
    r_pages = 16 if n_pages % 16 == 0 else 1
    steps = n_pages // r_pages

    def page_spec(r):
        return pl.BlockSpec((1, heads, page), lambda b, p, pt: (pt[b, n_pages - 1 - (p * r_pages + r)], 0, 0))

    grid_spec = pltpu.PrefetchScalarGridSpec(
        num_scalar_prefetch=1,
        grid=(bsz, steps),
        in_specs=[page_spec(r) for r in range(r_pages)],
        out_specs=pl.BlockSpec((1, r_pages, heads, page), lambda b, p, pt: (b, steps - 1 - p, 0, 0)),
        scratch_shapes=[pltpu.VMEM((heads, page), F32)],
    )
    return pl.pallas_call(
        functools.partial(_suffix_body, pages_per_step=r_pages),
        out_shape=jax.ShapeDtypeStruct((bsz, n_pages, heads, page), F32),
        grid_spec=grid_spec,
        compiler_params=_cparams(("parallel", "arbitrary")),
        name="suffix_log_forget",
    )(page_table, *([logf_t] * r_pages))


def _fox_decode_body(pt_ref, q_ref, *rest, n_heads, pages_per_step, page):
    del pt_ref
    r_pages = pages_per_step
    k_refs = rest[:r_pages]
    v_refs = rest[r_pages:2 * r_pages]
    (d_ref, cn_ref, cnt_ref, kn_ref, vn_ref, o_ref, m_ref, l_ref, acc_ref) = rest[2 * r_pages:]
    p_idx = pl.program_id(1)
    d = HEAD_DIM
    tq = q_ref.shape[1]
    scale = d ** -0.5
    half = n_heads // 2
    pairs = range(half)
    lanes = [slice(h * d, (h + 1) * d) for h in range(n_heads)]
    n_il = 2 * page

    @pl.when(p_idx == 0)
    def _():
        m_ref[...] = jnp.full_like(m_ref, -jnp.inf)
        l_ref[...] = jnp.zeros_like(l_ref)
        acc_ref[...] = jnp.zeros_like(acc_ref)

    def stack(top, bottom):
        return jnp.concatenate([top, bottom], axis=0)

    def update(s, pv_of):
        m_old = [m_ref[:, lanes[p]] for p in pairs]
        m_new = [jnp.maximum(m_old[p], jnp.max(s[p], axis=-1, keepdims=True)) for p in pairs]
        pr = [jnp.exp(s[p] - _lane_tiles(m_new[p], s[p].shape[1] // LANES)) for p in pairs]
        alpha = [jnp.exp(m_old[p] - m_new[p]) for p in pairs]
        pv = [pv_of(p, pr[p].astype(BF16)) for p in pairs]
        for p in pairs:
            l_ref[:, lanes[p]] = alpha[p] * l_ref[:, lanes[p]] + jnp.sum(pr[p], axis=-1, keepdims=True)
            acc_ref[:, lanes[p]] = alpha[p] * acc_ref[:, lanes[p]] + pv[p]
            m_ref[:, lanes[p]] = m_new[p]

    q = q_ref[0]
    qp = [stack(q[:, lanes[p]], q[:, lanes[p + half]]).astype(BF16) for p in pairs]
    cq = [stack(cn_ref[0, :, p:p + 1], cn_ref[0, :, p + half:p + half + 1]) for p in pairs]

    def pair_rows(ref, p):
        return ref[0, pl.ds(p, n_il, stride=half), :].astype(BF16)

    n_keys = r_pages * n_il
    row = lax.broadcasted_iota(jnp.int32, (2 * tq, n_keys), 0)
    col = lax.broadcasted_iota(jnp.int32, (2 * tq, n_keys), 1)
    own = (col & 1) == (row >= tq).astype(jnp.int32)

    s = []
    for p in pairs:
        parts = [lax.dot_general(qp[p], pair_rows(k_refs[r], p), NT_DIMS, preferred_element_type=F32) * scale
                 + (cq[p] + d_ref[0, r, p:p + 1, :]) for r in range(r_pages)]
        sp = parts[0] if r_pages == 1 else jnp.concatenate(parts, axis=1)
        s.append(jnp.where(own, sp, -jnp.inf))

    def pv_past(p, pb):
        acc = jnp.dot(pb[:, 0:n_il], pair_rows(v_refs[0], p), preferred_element_type=F32)
        for r in range(1, r_pages):
            acc = acc + jnp.dot(pb[:, r * n_il:(r + 1) * n_il], pair_rows(v_refs[r], p),
                                preferred_element_type=F32)
        return acc

    update(s, pv_past)

    @pl.when(p_idx == pl.num_programs(1) - 1)
    def _():
        n_new = kn_ref.shape[1]
        r_i = lax.broadcasted_iota(jnp.int32, (tq, n_new), 0)
        c_i = lax.broadcasted_iota(jnp.int32, (tq, n_new), 1)
        causal = stack(c_i <= r_i, c_i <= r_i)

        def logits_new(p):
            top = lax.dot_general(qp[p], kn_ref[0, :, lanes[p]].astype(BF16), NT_DIMS,
                                  preferred_element_type=F32)[:tq]
            bot = lax.dot_general(qp[p], kn_ref[0, :, lanes[p + half]].astype(BF16), NT_DIMS,
                                  preferred_element_type=F32)[tq:]
            bias = stack(cn_ref[0, :, p:p + 1] - cnt_ref[0, p:p + 1, :],
                         cn_ref[0, :, p + half:p + half + 1] - cnt_ref[0, p + half:p + half + 1, :])
            return jnp.where(causal, stack(top, bot) * scale + bias, -jnp.inf)

        def pv_new(p, pb):
            top = jnp.dot(pb, vn_ref[0, :, lanes[p]].astype(BF16), preferred_element_type=F32)[:tq]
            bot = jnp.dot(pb, vn_ref[0, :, lanes[p + half]].astype(BF16), preferred_element_type=F32)[tq:]
            return stack(top, bot)

        update([logits_new(p) for p in pairs], pv_new)
        o = acc_ref[...] / l_ref[...]
        for p in pairs:
            o_ref[0, :, lanes[p]] = o[:tq, lanes[p]].astype(o_ref.dtype)
            o_ref[0, :, lanes[p + half]] = o[tq:, lanes[p]].astype(o_ref.dtype)


def fox_attention_decode(q, cache_k, cache_v, d_suffix, cn, cnt, k_new, v_new, page_table):
    bsz, tq, w = q.shape
    heads = w // HEAD_DIM
    assert heads % 2 == 0
    half = heads // 2
    rows = cache_k.shape[1]
    page = rows // heads
    n_pages = page_table.shape[1]
    tn = k_new.shape[1]
    r_pages = 4 if n_pages % 4 == 0 else 1
    steps = n_pages // r_pages
    d_pairs = d_suffix.reshape(bsz, n_pages, 2, half, page).transpose(0, 1, 3, 4, 2).reshape(
        bsz, n_pages, half, 2 * page)

    def page_spec(r):
        return pl.BlockSpec((1, rows, HEAD_DIM), lambda b, p, pt: (pt[b, p * r_pages + r], 0, 0))

    whole = lambda shape: pl.BlockSpec((1,) + shape, lambda b, p, pt: (b, 0, 0))
    grid_spec = pltpu.PrefetchScalarGridSpec(
        num_scalar_prefetch=1,
        grid=(bsz, steps),
        in_specs=[whole((tq, w))] + [page_spec(r) for r in range(r_pages)] * 2
                 + [pl.BlockSpec((1, r_pages, half, 2 * page), lambda b, p, pt: (b, p, 0, 0)),
                    pl.BlockSpec((1, tq, LANES), lambda b, p, pt: (b, 0, 0)),
                    whole((LANES, tn)), whole((tn, w)), whole((tn, w))],
        out_specs=pl.BlockSpec((1, tq, w), lambda b, p, pt: (b, 0, 0)),
        scratch_shapes=[pltpu.VMEM((2 * tq, w // 2), F32)] * 3,
    )
    return pl.pallas_call(
        functools.partial(_fox_decode_body, n_heads=heads, pages_per_step=r_pages, page=page),
        out_shape=jax.ShapeDtypeStruct((bsz, tq, w), BF16),
        grid_spec=grid_spec,
        compiler_params=_cparams(("parallel", "arbitrary")),
        name="fox_attention_decode",
    )(page_table, q, *([cache_k] * r_pages), *([cache_v] * r_pages), d_pairs, cn, cnt, k_new, v_new)


def _pad_rows(x, rows):
    pad = [(0, 0)] * x.ndim
    pad[1] = (0, rows - x.shape[1])
    return jnp.pad(x, pad)


def _trunk(hp, hs, dims_p, dims_s, mem_p, mem_s, hg_s0, decode, wts):
    d_model = hp.shape[1]
    tok_w = wts["tok_w"]
    mem_w = d_model - tok_w
    mem_heads = mem_p[0].shape[3]
    to4 = lambda a: a.reshape(a.shape[:3] + (mem_w,))
    mem_p = (to4(mem_p[0]), to4(mem_p[1]))
    mem_s = (to4(mem_s[0]), to4(mem_s[1]))
    n_a = wts["w_in_a"].shape[0]
    depth = wts["w_out"].shape[0]
    (bp, tp), (bs, ts) = dims_p, dims_s
    states_p, states_s = [], []
    hn_p = rmsnorm_bf16(hp, wts["norm_pre_mix"][0])
    hn_s = rmsnorm_bf16(hs, wts["norm_pre_mix"][0])
    kv_p = kv_s = None
    hkv_p = hkv_s = None
    for layer in range(depth):
        cast_jobs = [(wts["w_out"], layer), (wts["w_mlp_up"], layer), (wts["w_mlp_down"], layer)]
        if layer == n_a:
            wkv = wts["w_kv"]
            k_p, k_s = matmul(hkv_p, hkv_s, wkv, 0, tok_w, trans_w=True)
            v_p, v_s = matmul(hkv_p, hkv_s, wkv, tok_w, tok_w, trans_w=True)
            lf_p, lf_s = matmul(hkv_p, hkv_s, wts["w_fg"], 0, LANES, kind="logsig", vecs=(wts["b_fg"],),
                                trans_w=True)
            kv_p, kv_s = (k_p, v_p, lf_p), (k_s, v_s, lf_s)
        if layer < n_a:
            w_in = wts["w_in_a"]
            q = matmul(hn_p, hn_s, w_in, 0, tok_w, kind="silu", layer=layer)
            f = matmul(hn_p, hn_s, w_in, tok_w, tok_w, kind="fgate", vecs=(wts["lb"][layer],), layer=layer)
            v = matmul(hn_p, hn_s, w_in, 2 * tok_w, tok_w, layer=layer)
            g = matmul(hn_p, hn_s, w_in, 3 * tok_w, tok_w, kind="silu", layer=layer)
            zm_p, zm_s = matmul(hn_p, hn_s, w_in, 4 * tok_w, mem_w, layer=layer)
            gain = wts["hg_out_norm"][layer]
            shp_p, shp_s = (bp, tp, tok_w), (bs, ts, tok_w)
            o_p, st_p, w_bf = hgrn2(q[0].reshape(shp_p), f[0].reshape(shp_p), v[0].reshape(shp_p),
                                    g[0].reshape(shp_p), gain, None, cast_jobs=cast_jobs)
            o_s, st_s, _ = hgrn2(q[1].reshape(shp_s), f[1].reshape(shp_s), v[1].reshape(shp_s), g[1].reshape(shp_s),
                                 gain, hg_s0[layer], wts["t_valid"])
            states_p.append(st_p)
            states_s.append(st_s)
        else:
            w_in = wts["w_in_b"]
            q_p, q_s = matmul(hn_p, hn_s, w_in, 0, tok_w, layer=layer - n_a)
            zm_p, zm_s = matmul(hn_p, hn_s, w_in, tok_w, mem_w, layer=layer - n_a)
            c, ct = cumsum_tokens(kv_p[2].reshape(bp, tp, LANES))
            o_p, w_bf = fox_attention_prompt(q_p.reshape(bp, tp, tok_w), kv_p[0].reshape(bp, tp, tok_w),
                                             kv_p[1].reshape(bp, tp, tok_w), c, ct, cast_jobs=cast_jobs)
            cn, cnt = cumsum_tokens(_pad_rows(kv_s[2].reshape(bs, ts, LANES), LANES))
            kn = _pad_rows(kv_s[0].reshape(bs, ts, tok_w), LANES)
            vn = _pad_rows(kv_s[1].reshape(bs, ts, tok_w), LANES)
            o_s = fox_attention_decode(q_s.reshape(bs, ts, tok_w), decode["cache_k"], decode["cache_v"],
                                       decode["d_suffix"], cn, cnt, kn, vn, decode["page_table"])
        om_p = memory_attention(zm_p.reshape(bp, tp, mem_w), mem_p[0], mem_p[1], layer, mem_heads)
        om_s = memory_attention(zm_s.reshape(bs, ts, mem_w), mem_s[0], mem_s[1], layer, mem_heads)
        w_out_b, w_up_b, w_down_b = [w[None] for w in w_bf]
        mixed_p, mixed_s = matmul((o_p.reshape(bp * tp, tok_w), om_p.reshape(bp * tp, mem_w)),
                                  (o_s.reshape(bs * ts, tok_w), om_s.reshape(bs * ts, mem_w)),
                                  w_out_b, 0, d_model)
        hp, (hn_p,) = post_norm_residual(mixed_p, hp, wts["norm_post_mix"][layer], [wts["norm_pre_mlp"][layer]])
        hs, (hn_s,) = post_norm_residual(mixed_s, hs, wts["norm_post_mix"][layer], [wts["norm_pre_mlp"][layer]])
        u_p, u_s = matmul(hn_p, hn_s, w_up_b, 0, w_up_b.shape[2], kind="relu2", out_dtype=BF16)
        down_p, down_s = matmul_long_k(u_p, u_s, w_down_b, 0)
        nxt = []
        if layer + 1 < depth:
            nxt = [wts["norm_pre_mix"][layer + 1]] + ([wts["kv_norm"]] if layer + 1 == n_a else [])
        hp, hns_p = post_norm_residual(down_p, hp, wts["norm_post_mlp"][layer], nxt)
        hs, hns_s = post_norm_residual(down_s, hs, wts["norm_post_mlp"][layer], nxt)
        if nxt:
            hn_p, hn_s = hns_p[0], hns_s[0]
            if layer + 1 == n_a:
                hkv_p, hkv_s = hns_p[1], hns_s[1]
    return (hp, jnp.stack(states_p), kv_p), (hs, jnp.stack(states_s), kv_s)


def kernel(x_prompt, x_sample, state_hgrn, cache_k, cache_v, cache_logf, cache_mem_k, cache_mem_v, page_table, mem_prompt, norm_pre_mix, norm_post_mix, norm_pre_mlp, norm_post_mlp, w_in_a, hg_lb_logits, hg_out_norm, w_in_b, kv_norm, w_kv_shared, b_fgate, w_out, mem_norm, w_mem_kv, w_mlp_up, w_mlp_down):
    bp, tp, d_model = x_prompt.shape
    bs, ts, _ = x_sample.shape
    fox_heads, fox_dim = cache_k.shape[2], cache_k.shape[3]
    assert fox_dim == HEAD_DIM and state_hgrn.shape[-1] == HEAD_DIM
    tok_w = fox_heads * fox_dim
    mem_w = d_model - tok_w
    mem_len = mem_prompt.shape[1]
    depth = w_out.shape[0]
    n_pool, page = cache_k.shape[0], cache_k.shape[1]
    t_pad = max(SUBLANES, ts)

    lb_all = jnp.cumsum(jax.nn.softmax(hg_lb_logits.astype(F32), axis=0), axis=0)
    wts = {
        "tok_w": tok_w, "t_valid": ts,
        "norm_pre_mix": norm_pre_mix, "norm_post_mix": norm_post_mix,
        "norm_pre_mlp": norm_pre_mlp, "norm_post_mlp": norm_post_mlp,
        "kv_norm": kv_norm, "hg_out_norm": hg_out_norm, "lb": lb_all,
        "w_in_a": w_in_a, "w_in_b": w_in_b,
        "w_kv": w_kv_shared.T[None],
        "w_fg": jnp.pad(w_kv_shared.T[2 * tok_w:], ((0, LANES - fox_heads), (0, 0)))[None],
        "b_fg": jnp.pad(b_fgate.astype(F32), (0, LANES - fox_heads)),
        "w_out": w_out, "w_mlp_up": w_mlp_up, "w_mlp_down": w_mlp_down,
    }

    mem_n = rmsnorm_bf16(mem_prompt.reshape(bp * mem_len, d_model), mem_norm)
    mem_k_flat = matmul(mem_n, None, w_mem_kv, 0, mem_w, all_layers=True)
    mem_v_flat = matmul(mem_n, None, w_mem_kv, mem_w, mem_w, all_layers=True)
    mem_heads = cache_mem_k.shape[3]
    mem_shape = (depth, bp, mem_len, mem_heads, mem_w // mem_heads)
    mem_k_prompt = mem_k_flat.reshape(mem_shape)
    mem_v_prompt = mem_v_flat.reshape(mem_shape)

    decode = {
        "cache_k": cache_k.reshape(n_pool, page * fox_heads, fox_dim),
        "cache_v": cache_v.reshape(n_pool, page * fox_heads, fox_dim),
        "d_suffix": suffix_log_forget(cache_logf.transpose(0, 2, 1), page_table),
        "page_table": page_table,
    }
    xs = _pad_rows(x_sample, t_pad).reshape(bs * t_pad, d_model)
    (y_p, hg_p, (k_p, v_p, lf_p)), (y_s, hg_s, (k_s, v_s, lf_s)) = _trunk(
        x_prompt.reshape(bp * tp, d_model), xs, (bp, tp), (bs, t_pad),
        (mem_k_prompt, mem_v_prompt), (cache_mem_k, cache_mem_v), state_hgrn, decode, wts)

    def unpad(a, width):
        return a.reshape(bs, t_pad, width)[:, :ts]

    return (y_p.reshape(bp, tp, d_model),
            unpad(y_s, d_model),
            hg_p.astype(state_hgrn.dtype),
            hg_s.astype(state_hgrn.dtype),
            k_p.reshape(bp, tp, fox_heads, fox_dim),
            v_p.reshape(bp, tp, fox_heads, fox_dim),
            lf_p[:, :fox_heads].reshape(bp, tp, fox_heads),
            unpad(k_s, tok_w).reshape(bs, ts, fox_heads, fox_dim),
            unpad(v_s, tok_w).reshape(bs, ts, fox_heads, fox_dim),
            unpad(lf_s, LANES)[:, :, :fox_heads],
            mem_k_prompt, mem_v_prompt)
```
